```python
import math
import jax, jax.numpy as jnp
from jax import lax
import numpy as np

D_MODEL = 1024
BATCH = 8
SEQ = 4096
DEPTH = 2

GRID_W = 64
CTX_LEN = 256
HEAD_DIM = 64
N_HEADS_TOTAL = D_MODEL // HEAD_DIM
A_HEADS = N_HEADS_TOTAL // 2
A_KV_HEADS = A_HEADS // 4
A_REP = A_HEADS // A_KV_HEADS
B_HEADS = N_HEADS_TOTAL // 4
C_HEADS = N_HEADS_TOTAL // 4
DIFF_DIM = HEAD_DIM // 2
WIN_H_MAX = 8
WIN_W = 16
Q_BLOCK = 128
FFN_DIM = ((8 * D_MODEL // 3 + 127) // 128) * 128
ROPE_THETA = 10000.0
EPS = 1e-6
IN_SIZES = (A_HEADS * HEAD_DIM, A_KV_HEADS * HEAD_DIM, A_KV_HEADS * HEAD_DIM,
            B_HEADS * HEAD_DIM, B_HEADS * HEAD_DIM, B_HEADS * HEAD_DIM,
            C_HEADS * HEAD_DIM, C_HEADS * HEAD_DIM, C_HEADS * HEAD_DIM)
IN_COLS = A_HEADS * HEAD_DIM + 2 * A_KV_HEADS * HEAD_DIM + 3 * B_HEADS * HEAD_DIM + 3 * C_HEADS * HEAD_DIM

kernel_name = "hybrid_gqa_natten_diffattn_convffn_prefix"


def rms_norm(x, g):
    xf = x.astype(jnp.float32)
    y = xf * lax.rsqrt(jnp.mean(xf * xf, axis=-1, keepdims=True) + EPS)
    return y.astype(x.dtype) * g


def modulate(h, shift, scale):
    return h * (1.0 + scale) + shift


def heads(t, n):
    b, s, _ = t.shape
    return t.reshape(b, s, n, -1).transpose(0, 2, 1, 3)


def diff_heads(t):
    b, s, _ = t.shape
    return t.reshape(b, s, C_HEADS, 2, DIFF_DIM).transpose(0, 2, 3, 1, 4)


def merge_heads(o):
    b, h, s, d = o.shape
    return o.transpose(0, 2, 1, 3).reshape(b, s, h * d)


def axial_rope_angles(L, dim):
    half = dim // 2
    inv = ROPE_THETA ** (-jnp.arange(0, half, 2, dtype=jnp.float32) / half)
    t = jnp.arange(L, dtype=jnp.int32)
    row = (t // GRID_W).astype(jnp.float32)
    col = (t % GRID_W).astype(jnp.float32)
    ang = jnp.concatenate([row[:, None] * inv, col[:, None] * inv], axis=-1)
    return jnp.cos(ang), jnp.sin(ang)


def apply_rope(x, cos, sin):
    x1, x2 = x[..., 0::2], x[..., 1::2]
    c = cos.astype(x.dtype)
    s = sin.astype(x.dtype)
    return jnp.stack([x1 * c - x2 * s, x1 * s + x2 * c], axis=-1).reshape(x.shape)


def sweep_blocks(fn, q, axis, out_axis):
    s = q.shape[axis]
    n = s // Q_BLOCK
    qb = jnp.moveaxis(q.reshape(q.shape[:axis] + (n, Q_BLOCK) + q.shape[axis + 1:]), axis, 0)
    o = jnp.moveaxis(lax.map(fn, qb), 0, out_axis)
    return o.reshape(o.shape[:out_axis] + (s,) + o.shape[out_axis + 2:])


def softmax_attend(q, kvs, s_spec, o_spec, scale):
    s = jnp.concatenate([jnp.einsum(s_spec, q, k, preferred_element_type=jnp.float32) for k, _ in kvs], axis=-1) * scale
    p = jax.nn.softmax(s, axis=-1)
    bounds = np.cumsum([k.shape[-2] for k, _ in kvs])[:-1].tolist()
    parts = jnp.split(p, bounds, axis=-1)
    terms = [jnp.einsum(o_spec, pi.astype(v.dtype), v) for pi, (_, v) in zip(parts, kvs)]
    return sum(terms[1:], terms[0])


def diff_attend(q, kvs, lam, scale):
    s = jnp.concatenate([jnp.einsum('bhmqd,bhmkd->bhmqk', q, k, preferred_element_type=jnp.float32)
                         for k, _ in kvs], axis=-1) * scale
    p = jax.nn.softmax(s, axis=-1)
    w = p[:, :, 0] - lam * p[:, :, 1]
    bounds = np.cumsum([k.shape[-2] for k, _ in kvs])[:-1].tolist()
    parts = jnp.split(w, bounds, axis=-1)
    terms = [jnp.einsum('bhqk,bhkd->bhqd', wi.astype(v.dtype), v) for wi, (_, v) in zip(parts, kvs)]
    return sum(terms[1:], terms[0])


def neighbourhood_attend(q, k, v, k_ctx, v_ctx, rpb):
    b, h, L, d = q.shape
    rows = L // GRID_W
    win_h = min(WIN_H_MAX, rows)
    n_win = win_h * WIN_W
    scale = d ** -0.5
    qg = q.reshape(b, h, rows, GRID_W, d)
    kg = k.reshape(b, h, rows, GRID_W, d)
    vg = v.reshape(b, h, rows, GRID_W, d)
    col = jnp.arange(GRID_W)
    col_idx = jnp.clip(col - WIN_W // 2, 0, GRID_W - WIN_W)[:, None] + jnp.arange(WIN_W)[None, :]
    col_bias = rpb[:, :, col_idx - col[:, None] + WIN_W - 1]

    def one_row(r):
        rs = jnp.clip(r - win_h // 2, 0, rows - win_h)
        q_r = lax.dynamic_index_in_dim(qg, r, axis=2, keepdims=False)
        k_win = lax.dynamic_slice_in_dim(kg, rs, win_h, axis=2)[:, :, :, col_idx]
        v_win = lax.dynamic_slice_in_dim(vg, rs, win_h, axis=2)[:, :, :, col_idx]
        row_off = rs + jnp.arange(win_h) - r
        bias = col_bias[:, row_off + WIN_H_MAX - 1].transpose(0, 2, 1, 3)
        s_win = jnp.einsum('bhqd,bhiqjd->bhqij', q_r, k_win, preferred_element_type=jnp.float32) * scale + bias
        s_ctx = jnp.einsum('bhqd,bhkd->bhqk', q_r, k_ctx, preferred_element_type=jnp.float32) * scale
        p = jax.nn.softmax(jnp.concatenate([s_win.reshape(b, h, GRID_W, n_win), s_ctx], axis=-1), axis=-1)
        p_win = p[..., :n_win].reshape(b, h, GRID_W, win_h, WIN_W).astype(v.dtype)
        p_ctx = p[..., n_win:].astype(v.dtype)
        return (jnp.einsum('bhqij,bhiqjd->bhqd', p_win, v_win)
                + jnp.einsum('bhqk,bhkd->bhqd', p_ctx, v_ctx))

    o = lax.map(one_row, jnp.arange(rows))
    return o.transpose(1, 2, 0, 3, 4).reshape(b, h, L, d)


def mixer_a(q, k, v, q_ctx, k_ctx, v_ctx, gq, gk, cos, sin, with_ctx):
    q = apply_rope(rms_norm(heads(q, A_HEADS), gq), cos, sin)
    k = apply_rope(rms_norm(heads(k, A_KV_HEADS), gk), cos, sin)
    v = heads(v, A_KV_HEADS)
    k_ctx = rms_norm(heads(k_ctx, A_KV_HEADS), gk)
    v_ctx = heads(v_ctx, A_KV_HEADS)
    b, _, L, d = q.shape
    scale = d ** -0.5
    s_spec, o_spec = 'bgrqd,bgkd->bgrqk', 'bgrqk,bgkd->bgrqd'
    qg = q.reshape(b, A_KV_HEADS, A_REP, L, d)
    o = sweep_blocks(lambda qb: softmax_attend(qb, ((k, v), (k_ctx, v_ctx)), s_spec, o_spec, scale), qg, 3, 3)
    out = merge_heads(o.reshape(b, A_HEADS, L, d))
    out_ctx = None
    if with_ctx:
        qc = rms_norm(heads(q_ctx, A_HEADS), gq)
        lc = qc.shape[2]
        oc = softmax_attend(qc.reshape(b, A_KV_HEADS, A_REP, lc, d), ((k_ctx, v_ctx),), s_spec, o_spec, scale)
        out_ctx = merge_heads(oc.reshape(b, A_HEADS, lc, d))
    return out, out_ctx


def mixer_b(q, k, v, q_ctx, k_ctx, v_ctx, gq, gk, rpb, with_ctx):
    q = rms_norm(heads(q, B_HEADS), gq)
    k = rms_norm(heads(k, B_HEADS), gk)
    v = heads(v, B_HEADS)
    k_ctx = rms_norm(heads(k_ctx, B_HEADS), gk)
    v_ctx = heads(v_ctx, B_HEADS)
    out = merge_heads(neighbourhood_attend(q, k, v, k_ctx, v_ctx, rpb))
    out_ctx = None
    if with_ctx:
        qc = rms_norm(heads(q_ctx, B_HEADS), gq)
        oc = softmax_attend(qc, ((k_ctx, v_ctx),), 'bhqd,bhkd->bhqk', 'bhqk,bhkd->bhqd', HEAD_DIM ** -0.5)
        out_ctx = merge_heads(oc)
    return out, out_ctx


def mixer_c(q, k, v, q_ctx, k_ctx, v_ctx, gq, gk, lq1, lk1, lq2, lk2, g_sub, lam_init, cos, sin, with_ctx):
    lam = (jnp.exp(jnp.sum(lq1.astype(jnp.float32) * lk1.astype(jnp.float32)))
           - jnp.exp(jnp.sum(lq2.astype(jnp.float32) * lk2.astype(jnp.float32))) + lam_init)
    q = apply_rope(rms_norm(diff_heads(q), gq), cos, sin)
    k = apply_rope(rms_norm(diff_heads(k), gk), cos, sin)
    v = heads(v, C_HEADS)
    k_ctx = rms_norm(diff_heads(k_ctx), gk)
    v_ctx = heads(v_ctx, C_HEADS)
    scale = DIFF_DIM ** -0.5
    o = sweep_blocks(lambda qb: diff_attend(qb, ((k, v), (k_ctx, v_ctx)), lam, scale), q, 3, 2)
    out = merge_heads(rms_norm(o, g_sub) * (1.0 - lam_init))
    out_ctx = None
    if with_ctx:
        qc = rms_norm(diff_heads(q_ctx), gq)
        oc = diff_attend(qc, ((k_ctx, v_ctx),), lam, scale)
        out_ctx = merge_heads(rms_norm(oc, g_sub) * (1.0 - lam_init))
    return out, out_ctx


def dwconv3(x, w, b):
    xp = jnp.pad(x, ((0, 0), (1, 1), (0, 0)))
    return xp[:, :-2] * w[0] + xp[:, 1:-1] * w[1] + xp[:, 2:] * w[2] + b


def conv_ffn(h, w_up, conv_w, conv_b, w_down):
    ug = dwconv3(h @ w_up, conv_w, conv_b)
    u, g = jnp.split(ug, 2, axis=-1)
    return (jax.nn.silu(g) * u) @ w_down


def setup_inputs(seed: int = 0) -> dict:
    key = jax.random.key(seed)
    ks = jax.random.split(key, 26)
    D = D_MODEL

    def nrm(k, shape, s):
        return jax.random.normal(k, shape, jnp.float32) * s

    return {
        "x": nrm(ks[0], (BATCH, SEQ, D), 1.0),
        "c": nrm(ks[1], (BATCH, D), 1.0),
        "ctx": nrm(ks[2], (BATCH, CTX_LEN, D), 1.0),
        "c_ctx": nrm(ks[3], (D,), 1.0),
        "w_ada": nrm(ks[4], (DEPTH, D, 6 * D), 0.5 * D ** -0.5),
        "b_ada": nrm(ks[5], (DEPTH, 6 * D), 0.01),
        "g_norm1": 1.0 + nrm(ks[6], (DEPTH, D), 0.01),
        "w_in": nrm(ks[7], (DEPTH, D, IN_COLS), D ** -0.5),
        "gq_a": 1.0 + nrm(ks[8], (DEPTH, HEAD_DIM), 0.01),
        "gk_a": 1.0 + nrm(ks[9], (DEPTH, HEAD_DIM), 0.01),
        "gq_b": 1.0 + nrm(ks[10], (DEPTH, HEAD_DIM), 0.01),
        "gk_b": 1.0 + nrm(ks[11], (DEPTH, HEAD_DIM), 0.01),
        "rpb_b": nrm(ks[12], (DEPTH, B_HEADS, 2 * WIN_H_MAX - 1, 2 * WIN_W - 1), 0.02),
        "gq_c": 1.0 + nrm(ks[13], (DEPTH, DIFF_DIM), 0.01),
        "gk_c": 1.0 + nrm(ks[14], (DEPTH, DIFF_DIM), 0.01),
        "lambda_q1": nrm(ks[15], (DEPTH, DIFF_DIM), 0.1),
        "lambda_k1": nrm(ks[16], (DEPTH, DIFF_DIM), 0.1),
        "lambda_q2": nrm(ks[17], (DEPTH, DIFF_DIM), 0.1),
        "lambda_k2": nrm(ks[18], (DEPTH, DIFF_DIM), 0.1),
        "g_subln": 1.0 + nrm(ks[19], (DEPTH, 2 * DIFF_DIM), 0.01),
        "w_out": nrm(ks[20], (DEPTH, D, D), D ** -0.5),
        "g_norm2": 1.0 + nrm(ks[21], (DEPTH, D), 0.01),
        "w_up": nrm(ks[22], (DEPTH, D, 2 * FFN_DIM), D ** -0.5),
        "conv_w": nrm(ks[23], (DEPTH, 3, 2 * FFN_DIM), 3 ** -0.5),
        "conv_b": nrm(ks[24], (DEPTH, 2 * FFN_DIM), 0.01),
        "w_down": nrm(ks[25], (DEPTH, FFN_DIM, D), FFN_DIM ** -0.5),
    }


def reference(x, c, ctx, c_ctx, w_ada, b_ada, g_norm1, w_in, gq_a, gk_a, gq_b, gk_b, rpb_b, gq_c, gk_c,
              lambda_q1, lambda_k1, lambda_q2, lambda_k2, g_subln, w_out, g_norm2, w_up, conv_w, conv_b, w_down):
    L = x.shape[1]
    cos_a, sin_a = axial_rope_angles(L, HEAD_DIM)
    cos_c, sin_c = axial_rope_angles(L, DIFF_DIM)
    silu_c = jax.nn.silu(c)
    silu_cc = jax.nn.silu(c_ctx)
    bounds = np.cumsum(IN_SIZES)[:-1].tolist()
    for l in range(DEPTH):
        with_ctx = l < DEPTH - 1
        lam_init = 0.8 - 0.6 * math.exp(-0.3 * l)
        mod_x = (silu_c @ w_ada[l] + b_ada[l])[:, None, :]
        mod_c = (silu_cc @ w_ada[l] + b_ada[l])[None, None, :]
        sh1_x, sc1_x, gt1_x, sh2_x, sc2_x, gt2_x = jnp.split(mod_x, 6, axis=-1)
        sh1_c, sc1_c, gt1_c, sh2_c, sc2_c, gt2_c = jnp.split(mod_c, 6, axis=-1)

        hx = modulate(rms_norm(x, g_norm1[l]), sh1_x, sc1_x)
        hc = modulate(rms_norm(ctx, g_norm1[l]), sh1_c, sc1_c)
        qa, ka, va, qb, kb, vb, qc, kc, vc = jnp.split(hx @ w_in[l], bounds, axis=-1)
        qa_t, ka_t, va_t, qb_t, kb_t, vb_t, qc_t, kc_t, vc_t = jnp.split(hc @ w_in[l], bounds, axis=-1)

        oa, oa_t = mixer_a(qa, ka, va, qa_t, ka_t, va_t, gq_a[l], gk_a[l], cos_a, sin_a, with_ctx)
        ob, ob_t = mixer_b(qb, kb, vb, qb_t, kb_t, vb_t, gq_b[l], gk_b[l], rpb_b[l], with_ctx)
        oc, oc_t = mixer_c(qc, kc, vc, qc_t, kc_t, vc_t, gq_c[l], gk_c[l], lambda_q1[l], lambda_k1[l],
                           lambda_q2[l], lambda_k2[l], g_subln[l], lam_init, cos_c, sin_c, with_ctx)

        x = x + gt1_x * (jnp.concatenate([oa, ob, oc], axis=-1) @ w_out[l])
        hx2 = modulate(rms_norm(x, g_norm2[l]), sh2_x, sc2_x)
        x = x + gt2_x * conv_ffn(hx2, w_up[l], conv_w[l], conv_b[l], w_down[l])

        if with_ctx:
            ctx = ctx + gt1_c * (jnp.concatenate([oa_t, ob_t, oc_t], axis=-1) @ w_out[l])
            hc2 = modulate(rms_norm(ctx, g_norm2[l]), sh2_c, sc2_c)
            ctx = ctx + gt2_c * conv_ffn(hc2, w_up[l], conv_w[l], conv_b[l], w_down[l])
    return x
```

```python
import functools
import math

import numpy as np
import jax
import jax.numpy as jnp
from jax import lax
from jax.experimental import pallas as pl
from jax.experimental.pallas import tpu as pltpu

F32 = jnp.float32
BF16 = jnp.bfloat16

D_MODEL = 1024
SEQ = 4096
GRID_W = 64
GRID_ROWS = SEQ // GRID_W
CTX_LEN = 256
HEAD_DIM = 64
A_HEADS = 8
A_KV_HEADS = 2
A_REP = A_HEADS // A_KV_HEADS
B_HEADS = 4
C_HEADS = 4
DIFF_DIM = 32
WIN_H = 8
WIN_W = 16
FFN_DIM = 2816
ROPE_THETA = 10000.0
EPS = 1e-6
NEG = -1e30

LANES = 128
SUBLANES = 8
KC = 256
VMEM_LIMIT = 56 * 1024 * 1024

QA_W, QB_W, QC_W = A_HEADS * HEAD_DIM, B_HEADS * HEAD_DIM, C_HEADS * HEAD_DIM
KA_W, KB_W, KC_W = A_KV_HEADS * HEAD_DIM, QB_W, QC_W
Q_COLS = QA_W + QB_W + QC_W
K_COLS = KC_W + KB_W + KA_W
V_COLS = QB_W + QC_W + KA_W
NORM_COLS = Q_COLS + K_COLS
SLAB_COLS = Q_COLS + V_COLS
KT_ROWS = KC_W + KB_W + 2 * KA_W
IN_COLS = NORM_COLS + V_COLS


def _cparams(sem):
    return pltpu.CompilerParams(dimension_semantics=sem, vmem_limit_bytes=VMEM_LIMIT)


def _dot(a, b):
    return jnp.dot(a, b, preferred_element_type=F32)


def _mod_kernel(c_ref, w_ref, b_ref, o_ref):
    c = c_ref[...]
    a = c * (1.0 / (1.0 + jnp.exp(-c)))
    a_hi = a.astype(BF16)
    a_lo = (a - a_hi.astype(F32)).astype(BF16)
    w = w_ref[0]
    w_hi = w.astype(BF16)
    w_lo = (w - w_hi.astype(F32)).astype(BF16)
    o_ref[0] = _dot(a_hi, w_hi) + _dot(a_hi, w_lo) + _dot(a_lo, w_hi) + b_ref[0]


def _modulation(c_all, w_ada, b_ada):
    depth = w_ada.shape[0]
    n = w_ada.shape[2]
    nb = 1536
    return pl.pallas_call(
        _mod_kernel,
        out_shape=jax.ShapeDtypeStruct((depth, 16, n), F32),
        grid=(depth, n // nb),
        in_specs=[pl.BlockSpec((16, D_MODEL), lambda l, j: (0, 0)),
                  pl.BlockSpec((1, D_MODEL, nb), lambda l, j: (l, 0, j)),
                  pl.BlockSpec((1, 1, nb), lambda l, j: (l, 0, j))],
        out_specs=pl.BlockSpec((1, 16, nb), lambda l, j: (l, 0, j)),
        compiler_params=_cparams(("arbitrary", "arbitrary")),
        name="adaln_mod",
    )(c_all, w_ada, b_ada.reshape(depth, 1, n))


_CHUNKS = ([("q", 64, "a")] * 4 + [("q", 64, None)] * 2 + [("q", 32, "c")] * 2
           + [("k", 32, "c")] * 2 + [("k", 64, None)] * 2 + [("ka", 64, "a")])


def _swap_halves(n, group):
    half = group // 2
    lane = lax.broadcasted_iota(jnp.int32, n.shape, 1)
    first = (lane % group) < half
    return jnp.where(first, pltpu.roll(n, LANES - half, 1), pltpu.roll(n, half, 1))


def _inproj_kernel(x_ref, mod_ref, g1_ref, w_ref, gain_ref, e64_ref, e32_ref,
                   cosa_ref, sina_ref, cosc_ref, sinc_ref, slab_ref, kt_ref, *, T):
    x = x_ref[0]
    ms = jnp.mean(x * x, axis=-1, keepdims=True)
    xn = x * lax.rsqrt(ms + EPS) * g1_ref[...]
    h = (xn * (1.0 + mod_ref[0, 1:2, :]) + mod_ref[0, 0:1, :]).astype(BF16)
    acc = _dot(h, w_ref[...])
    k_row = 0
    for ci, (kind, group, rope) in enumerate(_CHUNKS):
        a = acc[:, ci * LANES:(ci + 1) * LANES]
        y = a * a
        y_hi = y.astype(BF16)
        y_lo = (y - y_hi.astype(F32)).astype(BF16)
        e = e64_ref[...] if group == 64 else e32_ref[...]
        ss = _dot(y_hi, e) + _dot(y_lo, e)
        n = a * lax.rsqrt(ss * (1.0 / group) + EPS) * gain_ref[:, ci * LANES:(ci + 1) * LANES]
        if rope == "a":
            n = n * cosa_ref[...] + _swap_halves(n, 64) * sina_ref[...]
        elif rope == "c":
            n = n * cosc_ref[...] + _swap_halves(n, 32) * sinc_ref[...]
        if kind == "q":
            slab_ref[0, :, ci * LANES:(ci + 1) * LANES] = n.astype(BF16)
            continue
        nt = n.T.astype(BF16)
        for j in range(T // KC):
            blk = nt[:, j * KC:(j + 1) * KC]
            if kind == "k":
                kt_ref[0, j, k_row:k_row + LANES, :] = blk
            else:
                for g in range(A_KV_HEADS):
                    kg = blk[g * HEAD_DIM:(g + 1) * HEAD_DIM]
                    kt_ref[0, j, k_row + 2 * g * HEAD_DIM:k_row + (2 * g + 1) * HEAD_DIM, :] = kg
                    kt_ref[0, j, k_row + (2 * g + 1) * HEAD_DIM:k_row + (2 * g + 2) * HEAD_DIM, :] = kg
        k_row += LANES
    slab_ref[0, :, Q_COLS:] = acc[:, NORM_COLS:].astype(BF16)


def _inproj(x, mod_l, mod_row, g1, w, gain, e64, e32, tabs, T):
    b, s, _ = x.shape
    full = lambda shape: pl.BlockSpec(shape, lambda i, t: (0,) * len(shape))
    tab = pl.BlockSpec((T, LANES), lambda i, t: (t, 0))
    return pl.pallas_call(
        functools.partial(_inproj_kernel, T=T),
        out_shape=(jax.ShapeDtypeStruct((b, s, SLAB_COLS), BF16),
                   jax.ShapeDtypeStruct((b, s // KC, KT_ROWS, KC), BF16)),
        grid=(b, s // T),
        in_specs=[pl.BlockSpec((1, T, D_MODEL), lambda i, t: (i, t, 0)),
                  pl.BlockSpec((1, 6, D_MODEL), lambda i, t: (mod_row(i), 0, 0)),
                  full((1, D_MODEL)), full((D_MODEL, IN_COLS)), full((1, NORM_COLS)),
                  full((LANES, LANES)), full((LANES, LANES)), tab, tab, tab, tab],
        out_specs=(pl.BlockSpec((1, T, SLAB_COLS), lambda i, t: (i, t, 0)),
                   pl.BlockSpec((1, T // KC, KT_ROWS, KC), lambda i, t: (i, t, 0, 0))),
        compiler_params=_cparams(("parallel", "parallel")),
        name="qkv_proj",
    )(x, mod_l, g1, w, gain, e64, e32, *tabs)


def _lane_mask(shape, lo, hi):
    lane = lax.broadcasted_iota(jnp.int32, shape, 1)
    return (lane >= lo) & (lane < hi)


def _attend(q, kt_lat, kt_ctx, v_lat, v_ctx, krow, vcol, vhalf, n_lat, m_ref, acc_ref):
    m_rows = q.shape[0]
    vmask = _lane_mask((KC, LANES), vhalf * HEAD_DIM, (vhalf + 1) * HEAD_DIM)
    m_ref[0:m_rows] = jnp.full((m_rows, 1), NEG, F32)
    acc_ref[0:m_rows] = jnp.zeros((m_rows, LANES), F32)

    def step(kt, v):
        s = _dot(q, kt)
        m_prev = m_ref[0:m_rows]
        m_new = jnp.maximum(m_prev, jnp.max(s, axis=-1, keepdims=True))
        alpha = jnp.exp(m_prev - m_new)
        p = jnp.exp(s - m_new).astype(BF16)
        v_aug = jnp.where(vmask, v, jnp.ones_like(v))
        acc_ref[0:m_rows] = alpha * acc_ref[0:m_rows] + _dot(p, v_aug)
        m_ref[0:m_rows] = m_new

    if n_lat:
        def body(j, carry):
            row0 = pl.multiple_of(j * KC, KC)
            step(kt_lat[j, krow:krow + LANES, :], v_lat[pl.ds(row0, KC), vcol:vcol + LANES])
            return carry
        lax.fori_loop(0, n_lat, body, 0)
    step(kt_ctx[krow:krow + LANES, :], v_ctx[:, vcol:vcol + LANES])
    acc = acc_ref[0:m_rows]
    return acc / pltpu.roll(acc, HEAD_DIM, 1)


def _pair(even, odd):
    return jnp.where(_lane_mask(even.shape, 0, HEAD_DIM), even, odd)


def _gqa_kernel(*refs, n_lat, Tq):
    if n_lat:
        q_ref, ktl_ref, ktc_ref, vl_ref, vc_ref, o_ref, m_ref, acc_ref = refs
        kt_lat, v_lat = ktl_ref.at[0], vl_ref.at[0]
    else:
        q_ref, ktc_ref, vc_ref, o_ref, m_ref, acc_ref = refs
        kt_lat = v_lat = None
    for g in range(A_KV_HEADS):
        pieces = []
        for r in range(A_REP):
            hd = g * A_REP + r
            q128 = q_ref[0, :, (hd // 2) * LANES:(hd // 2 + 1) * LANES]
            keep = _lane_mask(q128.shape, (hd % 2) * HEAD_DIM, (hd % 2 + 1) * HEAD_DIM)
            pieces.append(jnp.where(keep, q128, jnp.zeros_like(q128)))
        q = jnp.concatenate(pieces, axis=0)
        out = _attend(q, kt_lat, ktc_ref.at[0, 0], v_lat, vc_ref.at[0], g * LANES, 0, g, n_lat, m_ref, acc_ref)
        tiles = []
        for r in range(A_REP):
            o = out[r * Tq:(r + 1) * Tq]
            tiles.append(o if r % 2 == g else pltpu.roll(o, HEAD_DIM, 1))
        for k in range(A_REP // 2):
            blk = g * (A_REP // 2) + k
            o_ref[0, :, blk * LANES:(blk + 1) * LANES] = _pair(tiles[2 * k], tiles[2 * k + 1]).astype(BF16)


def _diff_kernel(*refs, n_lat, Tq, lam_init):
    if n_lat:
        q_ref, ktl_ref, ktc_ref, vl_ref, vc_ref, lam_ref, gsub_ref, o_ref, m_ref, acc_ref = refs
        kt_lat, v_lat = ktl_ref.at[0], vl_ref.at[0]
    else:
        q_ref, ktc_ref, vc_ref, lam_ref, gsub_ref, o_ref, m_ref, acc_ref = refs
        kt_lat = v_lat = None
    lv = lam_ref[...]
    lam = (jnp.exp(jnp.sum(lv[0:1] * lv[1:2], axis=-1, keepdims=True))
           - jnp.exp(jnp.sum(lv[2:3] * lv[3:4], axis=-1, keepdims=True)) + lam_init)
    tiles = []
    for hd in range(C_HEADS):
        blk, half = hd // 2, hd % 2
        q128 = q_ref[0, :, blk * LANES:(blk + 1) * LANES]
        pieces = []
        for mth in range(2):
            lo = half * HEAD_DIM + mth * DIFF_DIM
            pieces.append(jnp.where(_lane_mask(q128.shape, lo, lo + DIFF_DIM), q128, jnp.zeros_like(q128)))
        q = jnp.concatenate(pieces, axis=0)
        out = _attend(q, kt_lat, ktc_ref.at[0, 0], v_lat, vc_ref.at[0], blk * LANES, blk * LANES, half,
                      n_lat, m_ref, acc_ref)
        o = out[0:Tq] - lam * out[Tq:2 * Tq]
        own = _lane_mask(o.shape, half * HEAD_DIM, (half + 1) * HEAD_DIM)
        ms = jnp.sum(jnp.where(own, o * o, 0.0), axis=-1, keepdims=True) * (1.0 / HEAD_DIM)
        tiles.append(o * lax.rsqrt(ms + EPS) * gsub_ref[...] * (1.0 - lam_init))
        if half == 1:
            o_ref[0, :, blk * LANES:(blk + 1) * LANES] = _pair(tiles[-2], tiles[-1]).astype(BF16)


def _plain_kernel(q_ref, ktc_ref, vc_ref, o_ref, m_ref, acc_ref, *, n_heads):
    tiles = []
    for hd in range(n_heads):
        blk, half = hd // 2, hd % 2
        q128 = q_ref[0, :, blk * LANES:(blk + 1) * LANES]
        keep = _lane_mask(q128.shape, half * HEAD_DIM, (half + 1) * HEAD_DIM)
        q = jnp.where(keep, q128, jnp.zeros_like(q128))
        tiles.append(_attend(q, None, ktc_ref.at[0, 0], None, vc_ref.at[0], blk * LANES, blk * LANES, half,
                             0, m_ref, acc_ref))
        if half == 1:
            o_ref[0, :, blk * LANES:(blk + 1) * LANES] = _pair(tiles[-2], tiles[-1]).astype(BF16)


def _flash_call(kernel, name, q_slab, q_blk, kt_blk, v_blk, out_w, Tq, m_rows, lat, ctx, extra=()):
    b, sq, _ = q_slab.shape
    slab_c, kt_c = ctx
    ins = [q_slab]
    specs = [pl.BlockSpec((1, Tq, q_blk[0]), lambda i, t: (i, t, q_blk[1]))]
    if lat is not None:
        slab_l, kt_l = lat
        nl = kt_l.shape[1]
    kt_ctx_spec = pl.BlockSpec((1, 1, kt_blk[0], KC), lambda i, t: (i, 0, kt_blk[1], 0))
    v_ctx_spec = pl.BlockSpec((1, CTX_LEN, v_blk[0]), lambda i, t: (i, 0, v_blk[1]))
    if lat is not None:
        ins += [kt_l, kt_c, slab_l, slab_c]
        specs += [pl.BlockSpec((1, nl, kt_blk[0], KC), lambda i, t: (i, 0, kt_blk[1], 0)), kt_ctx_spec,
                  pl.BlockSpec((1, slab_l.shape[1], v_blk[0]), lambda i, t: (i, 0, v_blk[1])), v_ctx_spec]
    else:
        ins += [kt_c, slab_c]
        specs += [kt_ctx_spec, v_ctx_spec]
    for arr in extra:
        ins.append(arr)
        specs.append(pl.BlockSpec(arr.shape, lambda i, t, nd=arr.ndim: (0,) * nd))
    return pl.pallas_call(
        kernel,
        out_shape=jax.ShapeDtypeStruct((b, sq, out_w), BF16),
        grid=(b, sq // Tq),
        in_specs=specs,
        out_specs=pl.BlockSpec((1, Tq, out_w), lambda i, t: (i, t, 0)),
        scratch_shapes=[pltpu.VMEM((m_rows, 1), F32), pltpu.VMEM((m_rows, LANES), F32)],
        compiler_params=_cparams(("parallel", "parallel")),
        name=name,
    )(*ins)


_QA_BLK, _QB_BLK, _QC_BLK = (QA_W, 0), (QB_W, 2), (QC_W, 3)
_VB_BLK, _VC_BLK, _VA_BLK = (QB_W, 4), (QC_W, 5), (KA_W, 12)
_KTC_BLK, _KTB_BLK, _KTA_BLK = (KC_W, 0), (KB_W, 1), (2 * KA_W, 2)


NB_ROWS = 8
NB_Q = NB_ROWS * GRID_W
NB_KROWS = 16
NB_K = NB_KROWS * GRID_W
NB_CHUNKS = NB_K // KC


def _nbr_kernel(q_ref, ktl_ref, ktc_ref, vl_ref, vc_ref, bias_ref, o_ref, s_ref):
    rb = pl.program_id(1)
    c0 = jnp.clip(2 * rb - 1, 0, SEQ // KC - NB_CHUNKS)
    tiles = []
    for hd in range(B_HEADS):
        blk, half = hd // 2, hd % 2
        rows = slice(blk * LANES, (blk + 1) * LANES)
        q128 = q_ref[0, :, rows]
        keep = _lane_mask(q128.shape, half * HEAD_DIM, (half + 1) * HEAD_DIM)
        q = jnp.where(keep, q128, jnp.zeros_like(q128))
        for j in range(NB_CHUNKS):
            s_ref[:, j * KC:(j + 1) * KC] = _dot(q, ktl_ref[0, c0 + j, rows, :]) + bias_ref[0, hd, :, j * KC:(j + 1) * KC]
        s_ref[:, NB_K:] = _dot(q, ktc_ref[0, 0, rows, :])
        s = s_ref[...]
        p = jnp.exp(s - jnp.max(s, axis=-1, keepdims=True)).astype(BF16)
        vmask = _lane_mask((KC, LANES), half * HEAD_DIM, (half + 1) * HEAD_DIM)
        vc = vc_ref[0, :, rows]
        acc = _dot(p[:, NB_K:], jnp.where(vmask, vc, jnp.ones_like(vc)))
        for j in range(NB_CHUNKS):
            row0 = pl.multiple_of((c0 + j) * KC, KC)
            v = vl_ref[0, pl.ds(row0, KC), rows]
            acc += _dot(p[:, j * KC:(j + 1) * KC], jnp.where(vmask, v, jnp.ones_like(v)))
        tiles.append(acc / pltpu.roll(acc, HEAD_DIM, 1))
        if half == 1:
            o_ref[0, :, rows] = _pair(tiles[-2], tiles[-1]).astype(BF16)


def _nbr_call(slab_l, kt_l, slab_c, kt_c, bias):
    b = slab_l.shape[0]
    nl = kt_l.shape[1]
    cls = lambda t: jnp.where(t == 0, 0, jnp.where(t == GRID_ROWS // NB_ROWS - 1, 2, 1))
    return pl.pallas_call(
        _nbr_kernel,
        out_shape=jax.ShapeDtypeStruct((b, SEQ, QB_W), BF16),
        grid=(b, GRID_ROWS // NB_ROWS),
        in_specs=[pl.BlockSpec((1, NB_Q, QB_W), lambda i, t: (i, t, _QB_BLK[1])),
                  pl.BlockSpec((1, nl, KB_W, KC), lambda i, t: (i, 0, _KTB_BLK[1], 0)),
                  pl.BlockSpec((1, 1, KB_W, KC), lambda i, t: (i, 0, _KTB_BLK[1], 0)),
                  pl.BlockSpec((1, SEQ, QB_W), lambda i, t: (i, 0, _VB_BLK[1])),
                  pl.BlockSpec((1, CTX_LEN, QB_W), lambda i, t: (i, 0, _VB_BLK[1])),
                  pl.BlockSpec((1, B_HEADS, NB_Q, NB_K), lambda i, t: (cls(t), 0, 0, 0))],
        out_specs=pl.BlockSpec((1, NB_Q, QB_W), lambda i, t: (i, t, 0)),
        scratch_shapes=[pltpu.VMEM((NB_Q, NB_K + CTX_LEN), F32)],
        compiler_params=_cparams(("parallel", "arbitrary")),
        name="nbr_attn",
    )(slab_l, kt_l, kt_c, slab_l, slab_c, bias)


def _nbr_bias_tables(rpb):
    n_blocks = GRID_ROWS // NB_ROWS
    sel_r = np.zeros((3, NB_ROWS, NB_KROWS, 2 * WIN_H - 1), np.float32)
    for cls, rbk in enumerate((0, 1, n_blocks - 1)):
        ws = min(max(NB_ROWS * rbk - WIN_H // 2, 0), GRID_ROWS - NB_KROWS)
        for rq in range(NB_ROWS):
            r = NB_ROWS * rbk + rq
            rs = min(max(r - WIN_H // 2, 0), GRID_ROWS - WIN_H)
            for kr in range(NB_KROWS):
                if rs <= ws + kr < rs + WIN_H:
                    sel_r[cls, rq, kr, ws + kr - r + WIN_H - 1] = 1.0
    sel_c = np.zeros((GRID_W, GRID_W, 2 * WIN_W - 1), np.float32)
    for c in range(GRID_W):
        cs = min(max(c - WIN_W // 2, 0), GRID_W - WIN_W)
        for kc in range(cs, cs + WIN_W):
            sel_c[c, kc, kc - c + WIN_W - 1] = 1.0
    valid = np.einsum("xqk,cd->xqckd", sel_r.sum(-1), sel_c.sum(-1)) > 0.5
    t = jnp.einsum("xqki,hij->xhqkj", jnp.asarray(sel_r), rpb, precision=lax.Precision.HIGHEST)
    t = jnp.einsum("xhqkj,cdj->xhqckd", t, jnp.asarray(sel_c), precision=lax.Precision.HIGHEST)
    t = jnp.where(jnp.asarray(valid)[:, None], t, NEG)
    return t.reshape(3, rpb.shape[0], NB_Q, NB_K)


def _outproj_kernel(x_ref, mod_ref, oa_ref, ob_ref, oc_ref, w_ref, o_ref):
    y = (_dot(oa_ref[0], w_ref[0:QA_W]) + _dot(ob_ref[0], w_ref[QA_W:QA_W + QB_W])
         + _dot(oc_ref[0], w_ref[QA_W + QB_W:]))
    o_ref[0] = x_ref[0] + mod_ref[0, 2:3, :] * y


def _outproj(x, mod_l, mod_row, oa, ob, oc, w, T):
    b, s, _ = x.shape
    tile = lambda wd: pl.BlockSpec((1, T, wd), lambda i, t: (i, t, 0))
    return pl.pallas_call(
        _outproj_kernel,
        out_shape=jax.ShapeDtypeStruct(x.shape, F32),
        grid=(b, s // T),
        in_specs=[tile(D_MODEL), pl.BlockSpec((1, 6, D_MODEL), lambda i, t: (mod_row(i), 0, 0)),
                  tile(QA_W), tile(QB_W), tile(QC_W),
                  pl.BlockSpec((D_MODEL, D_MODEL), lambda i, t: (0, 0))],
        out_specs=tile(D_MODEL),
        compiler_params=_cparams(("parallel", "parallel")),
        name="out_proj",
    )(x, mod_l, oa, ob, oc, w)


FFN_CHUNK = 256
HALO = SUBLANES


def _ffn_kernel(x_ref, xp_ref, xn_ref, mod_ref, g2_ref, wup_ref, cw_ref, cb_ref, wdn_ref, o_ref, acc_ref, *, T):
    t = pl.program_id(1)
    nt = pl.num_programs(1)
    x = x_ref[0]
    xa = jnp.concatenate([xp_ref[0], x, xn_ref[0]], axis=0)
    ms = jnp.mean(xa * xa, axis=-1, keepdims=True)
    h = xa * lax.rsqrt(ms + EPS) * g2_ref[...]
    h = h * (1.0 + mod_ref[0, 4:5, :]) + mod_ref[0, 3:4, :]
    row = lax.broadcasted_iota(jnp.int32, (T + 2 * HALO, 1), 0)
    inside = ((row >= HALO) | (t > 0)) & ((row < T + HALO) | (t < nt - 1))
    h = jnp.where(inside, h, 0.0).astype(BF16)

    def conv(y, col0):
        w = cw_ref[:, col0:col0 + FFN_CHUNK]
        return (y[HALO - 1:HALO - 1 + T] * w[0:1] + y[HALO:HALO + T] * w[1:2]
                + y[HALO + 1:HALO + 1 + T] * w[2:3] + cb_ref[:, col0:col0 + FFN_CHUNK])

    for j in range(FFN_DIM // FFN_CHUNK):
        cu, cg = j * FFN_CHUNK, FFN_DIM + j * FFN_CHUNK
        u = conv(_dot(h, wup_ref[:, cu:cu + FFN_CHUNK]), cu)
        g = conv(_dot(h, wup_ref[:, cg:cg + FFN_CHUNK]), cg)
        a = (g * (1.0 / (1.0 + jnp.exp(-g))) * u).astype(BF16)
        part = _dot(a, wdn_ref[cu:cu + FFN_CHUNK, :])
        if j == 0:
            acc_ref[...] = part
        else:
            acc_ref[...] += part
    o_ref[0] = x + mod_ref[0, 5:6, :] * acc_ref[...]


def _ffn(x, mod_l, mod_row, g2, w_up, conv_w, conv_b, w_down, T):
    b, s, _ = x.shape
    nh = T // HALO
    last = s // HALO - 1
    const = lambda shape: pl.BlockSpec(shape, lambda i, t: (0,) * len(shape), pipeline_mode=pl.Buffered(1))
    return pl.pallas_call(
        functools.partial(_ffn_kernel, T=T),
        out_shape=jax.ShapeDtypeStruct(x.shape, F32),
        grid=(b, s // T),
        in_specs=[pl.BlockSpec((1, T, D_MODEL), lambda i, t: (i, t, 0)),
                  pl.BlockSpec((1, HALO, D_MODEL), lambda i, t: (i, jnp.maximum(t * nh - 1, 0), 0)),
                  pl.BlockSpec((1, HALO, D_MODEL), lambda i, t: (i, jnp.minimum((t + 1) * nh, last), 0)),
                  pl.BlockSpec((1, 6, D_MODEL), lambda i, t: (mod_row(i), 0, 0)),
                  const((1, D_MODEL)), const((D_MODEL, 2 * FFN_DIM)), const((3, 2 * FFN_DIM)),
                  const((1, 2 * FFN_DIM)), const((FFN_DIM, D_MODEL))],
        out_specs=pl.BlockSpec((1, T, D_MODEL), lambda i, t: (i, t, 0)),
        scratch_shapes=[pltpu.VMEM((T, D_MODEL), F32)],
        compiler_params=_cparams(("parallel", "arbitrary")),
        name="conv_ffn",
    )(x, x, x, mod_l, g2, w_up, conv_w, conv_b, w_down)


def _split_pairs(w, n_groups, dim):
    lead = w.shape[:-1]
    return w.reshape(lead + (n_groups, dim // 2, 2)).swapaxes(-1, -2).reshape(lead + (n_groups * dim,))


def _rope_tables(dim):
    half = dim // 2
    inv = ROPE_THETA ** (-jnp.arange(0, half, 2, dtype=F32) / half)
    t = jnp.arange(SEQ, dtype=jnp.int32)
    row = (t // GRID_W).astype(F32)
    col = (t % GRID_W).astype(F32)
    ang = jnp.concatenate([row[:, None] * inv, col[:, None] * inv], axis=-1)
    cos = jnp.tile(jnp.cos(ang), (1, LANES // half))
    sin = jnp.tile(jnp.concatenate([-jnp.sin(ang), jnp.sin(ang)], axis=-1), (1, LANES // dim))
    return cos, sin


def _group_ones(group):
    idx = np.arange(LANES) // group
    return jnp.asarray((idx[:, None] == idx[None, :]).astype(np.float32), dtype=BF16)


def _layer_params(l, w_in, gq_a, gk_a, gq_b, gk_b, gq_c, gk_c):
    w = w_in[l]
    o = np.cumsum((0, QA_W, KA_W, KA_W, QB_W, QB_W, QB_W, QC_W, QC_W, QC_W))
    qa, ka, va, qb, kb, vb, qc, kc, vc = (w[:, o[i]:o[i + 1]] for i in range(9))
    wp = jnp.concatenate([_split_pairs(qa, A_HEADS, HEAD_DIM), qb, _split_pairs(qc, 2 * C_HEADS, DIFF_DIM),
                          _split_pairs(kc, 2 * C_HEADS, DIFF_DIM), kb, _split_pairs(ka, A_KV_HEADS, HEAD_DIM),
                          vb, vc, va], axis=1).astype(BF16)
    gqa = _split_pairs(gq_a[l], 1, HEAD_DIM) * HEAD_DIM ** -0.5
    gka = _split_pairs(gk_a[l], 1, HEAD_DIM)
    gqc = _split_pairs(gq_c[l], 1, DIFF_DIM) * DIFF_DIM ** -0.5
    gkc = _split_pairs(gk_c[l], 1, DIFF_DIM)
    gain = jnp.concatenate([jnp.tile(gqa, A_HEADS), jnp.tile(gq_b[l] * HEAD_DIM ** -0.5, B_HEADS),
                            jnp.tile(gqc, 2 * C_HEADS), jnp.tile(gkc, 2 * C_HEADS), jnp.tile(gk_b[l], B_HEADS),
                            jnp.tile(gka, A_KV_HEADS)])
    return wp, gain.reshape(1, NORM_COLS)


def kernel(x, c, ctx, c_ctx, w_ada, b_ada, g_norm1, w_in, gq_a, gk_a, gq_b, gk_b, rpb_b, gq_c, gk_c,
           lambda_q1, lambda_k1, lambda_q2, lambda_k2, g_subln, w_out, g_norm2, w_up, conv_w, conv_b, w_down):
    depth = w_ada.shape[0]
    batch = x.shape[0]
    assert x.shape == (batch, SEQ, D_MODEL) and ctx.shape == (batch, CTX_LEN, D_MODEL) and batch <= 8

    c_all = jnp.zeros((16, D_MODEL), F32).at[:batch].set(c).at[8].set(c_ctx)
    mod = _modulation(c_all, w_ada, b_ada).reshape(depth, 16, 6, D_MODEL)
    row_x = lambda i: i
    row_c = lambda i: 8

    cos_a, sin_a = _rope_tables(HEAD_DIM)
    cos_c, sin_c = _rope_tables(DIFF_DIM)
    tabs_x = (cos_a, sin_a, cos_c, sin_c)
    one = jnp.ones((CTX_LEN, LANES), F32)
    tabs_c = (one, jnp.zeros_like(one), one, jnp.zeros_like(one))
    e64, e32 = _group_ones(HEAD_DIM), _group_ones(DIFF_DIM)

    for l in range(depth):
        with_ctx = l < depth - 1
        lam_init = 0.8 - 0.6 * math.exp(-0.3 * l)
        mod_l = mod[l]
        wp, gain = _layer_params(l, w_in, gq_a, gk_a, gq_b, gk_b, gq_c, gk_c)
        g1 = g_norm1[l].reshape(1, D_MODEL)
        g2 = g_norm2[l].reshape(1, D_MODEL)
        lam_vec = jnp.zeros((SUBLANES, LANES), F32).at[0:4, :DIFF_DIM].set(
            jnp.stack([lambda_q1[l], lambda_k1[l], lambda_q2[l], lambda_k2[l]]))
        gsub = jnp.tile(g_subln[l], 2).reshape(1, LANES)
        w_o = w_out[l].astype(BF16)
        ffn_w = (w_up[l].astype(BF16), conv_w[l], conv_b[l].reshape(1, -1), w_down[l].astype(BF16))

        lat = _inproj(x, mod_l, row_x, g1, wp, gain, e64, e32, tabs_x, KC)
        cx = _inproj(ctx, mod_l, row_c, g1, wp, gain, e64, e32, tabs_c, KC)

        oa = _flash_call(functools.partial(_gqa_kernel, n_lat=SEQ // KC, Tq=256), "gqa_attn", lat[0],
                         _QA_BLK, _KTA_BLK, _VA_BLK, QA_W, 256, A_REP * 256, lat, cx)
        oc = _flash_call(functools.partial(_diff_kernel, n_lat=SEQ // KC, Tq=512, lam_init=lam_init), "diff_attn",
                         lat[0], _QC_BLK, _KTC_BLK, _VC_BLK, QC_W, 512, 2 * 512, lat, cx, (lam_vec, gsub))
        ob = _nbr_call(lat[0], lat[1], cx[0], cx[1], _nbr_bias_tables(rpb_b[l]))
        x = _outproj(x, mod_l, row_x, oa, ob, oc, w_o, 512)
        x = _ffn(x, mod_l, row_x, g2, *ffn_w, 256)

        if with_ctx:
            ta = _flash_call(functools.partial(_gqa_kernel, n_lat=0, Tq=CTX_LEN), "gqa_attn_ctx", cx[0],
                             _QA_BLK, _KTA_BLK, _VA_BLK, QA_W, CTX_LEN, A_REP * CTX_LEN, None, cx)
            tc = _flash_call(functools.partial(_diff_kernel, n_lat=0, Tq=CTX_LEN, lam_init=lam_init),
                             "diff_attn_ctx", cx[0], _QC_BLK, _KTC_BLK, _VC_BLK, QC_W, CTX_LEN, 2 * CTX_LEN,
                             None, cx, (lam_vec, gsub))
            tb = _flash_call(functools.partial(_plain_kernel, n_heads=B_HEADS), "plain_attn_ctx", cx[0],
                             _QB_BLK, _KTB_BLK, _VB_BLK, QB_W, CTX_LEN, CTX_LEN, None, cx)
            ctx = _outproj(ctx, mod_l, row_c, ta, tb, tc, w_o, CTX_LEN)
            ctx = _ffn(ctx, mod_l, row_c, g2, *ffn_w, CTX_LEN)
    return x
```

```python
import functools
import math

import numpy as np
import jax
import jax.numpy as jnp
from jax import lax
from jax.experimental import pallas as pl
from jax.experimental.pallas import tpu as pltpu

F32 = jnp.float32
BF16 = jnp.bfloat16

D_MODEL = 1024
SEQ = 4096
GRID_W = 64
GRID_ROWS = SEQ // GRID_W
CTX_LEN = 256
HEAD_DIM = 64
A_HEADS = 8
A_KV_HEADS = 2
A_REP = A_HEADS // A_KV_HEADS
B_HEADS = 4
C_HEADS = 4
DIFF_DIM = 32
WIN_H = 8
WIN_W = 16
FFN_DIM = 2816
ROPE_THETA = 10000.0
EPS = 1e-6
NEG = -1e30
LOG2E = 1.4426950408889634

LANES = 128
SUBLANES = 8
KC = 256
VMEM_LIMIT = 56 * 1024 * 1024

QA_W, QB_W, QC_W = A_HEADS * HEAD_DIM, B_HEADS * HEAD_DIM, C_HEADS * HEAD_DIM
KA_W = A_KV_HEADS * HEAD_DIM
NAT_COLS = QB_W + QB_W + QC_W + KA_W
_QB_BLK, _VB_BLK, _KC_BLK, _KA_BLK = (QB_W, 0), (QB_W, 1), (QC_W, 2), (KA_W, 6)
TR_ROWS = QA_W + QC_W + QB_W + KA_W + QC_W
QT_ROWS = QA_W + QC_W
_QTA_BLK, _QTC_BLK = (QA_W, 0), (QC_W, 2)
ONES_ROWS = 16
AUG = HEAD_DIM + ONES_ROWS
VT_ROWS = C_HEADS * AUG + A_KV_HEADS * AUG
_VTC_BLK, _VTA_BLK = (C_HEADS * AUG, 0), (A_KV_HEADS * AUG, 2)


def _cparams(sem):
    return pltpu.CompilerParams(dimension_semantics=sem, vmem_limit_bytes=VMEM_LIMIT)


def _dot(a, b):
    return jnp.dot(a, b, preferred_element_type=F32)


def _lane_mask(shape, lo, hi):
    lane = lax.broadcasted_iota(jnp.int32, shape, 1)
    return (lane >= lo) & (lane < hi)


def _mod_kernel(c_ref, w_ref, b_ref, o_ref):
    c = c_ref[...]
    a = c * (1.0 / (1.0 + jnp.exp(-c)))
    a_hi = a.astype(BF16)
    a_lo = (a - a_hi.astype(F32)).astype(BF16)
    w = w_ref[0]
    w_hi = w.astype(BF16)
    w_lo = (w - w_hi.astype(F32)).astype(BF16)
    o_ref[0] = _dot(a_hi, w_hi) + _dot(a_hi, w_lo) + _dot(a_lo, w_hi) + b_ref[0]


def _modulation(c_all, w_ada, b_ada):
    depth = w_ada.shape[0]
    n = w_ada.shape[2]
    nb = 1536
    return pl.pallas_call(
        _mod_kernel,
        out_shape=jax.ShapeDtypeStruct((depth, 16, n), F32),
        grid=(depth, n // nb),
        in_specs=[pl.BlockSpec((16, D_MODEL), lambda l, j: (0, 0)),
                  pl.BlockSpec((1, D_MODEL, nb), lambda l, j: (l, 0, j)),
                  pl.BlockSpec((1, 1, nb), lambda l, j: (l, 0, j))],
        out_specs=pl.BlockSpec((1, 16, nb), lambda l, j: (l, 0, j)),
        compiler_params=_cparams(("arbitrary", "arbitrary")),
        name="adaln_mod",
    )(c_all, w_ada, b_ada.reshape(depth, 1, n))


_NAT_CHUNKS = [(64, None)] * 2 + [(None, None)] * 2 + [(32, "c")] * 2 + [(64, "a")]


def _swap_halves(n, group):
    half = group // 2
    lane = lax.broadcasted_iota(jnp.int32, n.shape, 1)
    first = (lane % group) < half
    return jnp.where(first, pltpu.roll(n, LANES - half, 1), pltpu.roll(n, half, 1))


def _norm_rope_t(a, gain, cos, sin):
    d = a.shape[0]
    ss = jnp.sum(a * a, axis=0, keepdims=True)
    n = a * lax.rsqrt(ss * (1.0 / d) + EPS) * gain
    if cos is None:
        return n
    x1, x2 = n[0:d // 2], n[d // 2:d]
    return jnp.concatenate([x1 * cos - x2 * sin, x1 * sin + x2 * cos], axis=0)


def _inproj_kernel(x_ref, mod_ref, g1_ref, wn_ref, wt_ref, gn_ref, gt_ref, e64_ref, e32_ref,
                   cosa_ref, sina_ref, cosc_ref, sinc_ref, cosat_ref, sinat_ref, cosct_ref, sinct_ref,
                   slab_ref, qt_ref, ktb_ref, vt_ref, *, T):
    x = x_ref[0]
    ms = jnp.mean(x * x, axis=-1, keepdims=True)
    xn = x * lax.rsqrt(ms + EPS) * g1_ref[...]
    h = (xn * (1.0 + mod_ref[0, 1:2, :]) + mod_ref[0, 0:1, :]).astype(BF16)

    nat = _dot(h, wn_ref[...])
    for ci, (group, rope) in enumerate(_NAT_CHUNKS):
        cols = slice(ci * LANES, (ci + 1) * LANES)
        a = nat[:, cols]
        if group is not None:
            y = a * a
            y_hi = y.astype(BF16)
            y_lo = (y - y_hi.astype(F32)).astype(BF16)
            e = e64_ref[...] if group == 64 else e32_ref[...]
            ss = _dot(y_hi, e) + _dot(y_lo, e)
            a = a * lax.rsqrt(ss * (1.0 / group) + EPS) * gn_ref[:, cols]
        if rope == "a":
            a = a * cosa_ref[...] + _swap_halves(a, 64) * sina_ref[...]
        elif rope == "c":
            a = a * cosc_ref[...] + _swap_halves(a, 32) * sinc_ref[...]
        slab_ref[0, :, cols] = a.astype(BF16)

    tr = lax.dot_general(wt_ref[...], h, (((1,), (1,)), ((), ())), preferred_element_type=F32)
    for hd in range(A_HEADS):
        rows = slice(hd * HEAD_DIM, (hd + 1) * HEAD_DIM)
        qt_ref[0, rows, :] = _norm_rope_t(tr[rows], gt_ref[rows, :], cosat_ref[...], sinat_ref[...]).astype(BF16)
    for sh in range(2 * C_HEADS):
        rows = slice(QA_W + sh * DIFF_DIM, QA_W + (sh + 1) * DIFF_DIM)
        qt_ref[0, rows, :] = _norm_rope_t(tr[rows], gt_ref[rows, :], cosct_ref[...], sinct_ref[...]).astype(BF16)
    ones = jnp.ones((ONES_ROWS, KC), BF16)
    for j in range(T // KC):
        tok = slice(j * KC, (j + 1) * KC)
        for hd in range(B_HEADS):
            rows = slice(QT_ROWS + hd * HEAD_DIM, QT_ROWS + (hd + 1) * HEAD_DIM)
            ktb_ref[0, j, hd * HEAD_DIM:(hd + 1) * HEAD_DIM, :] = _norm_rope_t(
                tr[rows, tok], gt_ref[rows, tok], None, None).astype(BF16)
        v0 = QT_ROWS + QB_W
        for i in range(A_KV_HEADS + C_HEADS):
            src = v0 + i * HEAD_DIM
            dst = (C_HEADS + i if i < A_KV_HEADS else i - A_KV_HEADS) * AUG
            vt_ref[0, j, dst:dst + HEAD_DIM, :] = tr[src:src + HEAD_DIM, tok].astype(BF16)
            vt_ref[0, j, dst + HEAD_DIM:dst + AUG, :] = ones


def _inproj(x, mod_l, mod_row, g1, wn, wt, gn, gt, e64, e32, tabs, T):
    b, s, _ = x.shape
    full = lambda shape: pl.BlockSpec(shape, lambda i, t: (0,) * len(shape))
    tab = pl.BlockSpec((T, LANES), lambda i, t: (t, 0))
    tab_t = lambda r: pl.BlockSpec((r, T), lambda i, t: (0, t))
    return pl.pallas_call(
        functools.partial(_inproj_kernel, T=T),
        out_shape=(jax.ShapeDtypeStruct((b, s, NAT_COLS), BF16),
                   jax.ShapeDtypeStruct((b, QT_ROWS, s), BF16),
                   jax.ShapeDtypeStruct((b, s // KC, QB_W, KC), BF16),
                   jax.ShapeDtypeStruct((b, s // KC, VT_ROWS, KC), BF16)),
        grid=(b, s // T),
        in_specs=[pl.BlockSpec((1, T, D_MODEL), lambda i, t: (i, t, 0)),
                  pl.BlockSpec((1, 6, D_MODEL), lambda i, t: (mod_row(i), 0, 0)),
                  full((1, D_MODEL)), full((D_MODEL, NAT_COLS)), full((TR_ROWS, D_MODEL)),
                  full((1, NAT_COLS)), full((QT_ROWS + QB_W, T)), full((LANES, LANES)), full((LANES, LANES)),
                  tab, tab, tab, tab,
                  tab_t(HEAD_DIM // 2), tab_t(HEAD_DIM // 2), tab_t(DIFF_DIM // 2), tab_t(DIFF_DIM // 2)],
        out_specs=(pl.BlockSpec((1, T, NAT_COLS), lambda i, t: (i, t, 0)),
                   pl.BlockSpec((1, QT_ROWS, T), lambda i, t: (i, 0, t)),
                   pl.BlockSpec((1, T // KC, QB_W, KC), lambda i, t: (i, t, 0, 0)),
                   pl.BlockSpec((1, T // KC, VT_ROWS, KC), lambda i, t: (i, t, 0, 0))),
        compiler_params=_cparams(("parallel", "parallel")),
        name="qkv_proj",
    )(x, mod_l, g1, wn, wt, gn, gt, e64, e32, *tabs)


def _flash_t(streams, k_all, vt_all, n_pairs, s_ref, m_ref, acc_ref):
    for i, (qt, _, _) in enumerate(streams):
        mq = qt.shape[1]
        m_ref[i, :, 0:mq] = jnp.full((1, mq), NEG, F32)
        acc_ref[i, :, 0:mq] = jnp.zeros((AUG, mq), F32)

    def scores(c, buf):
        row0 = pl.multiple_of(c * KC, KC)
        for i, (qt, kcol, _) in enumerate(streams):
            s_ref[buf, i, :, 0:qt.shape[1]] = _dot(k_all[pl.ds(row0, KC), kcol:kcol + LANES], qt)

    def update(c, buf):
        for i, (qt, _, vrow) in enumerate(streams):
            mq = qt.shape[1]
            s = s_ref[buf, i, :, 0:mq]
            m_prev = m_ref[i, :, 0:mq]
            m_new = jnp.maximum(m_prev, jnp.max(s, axis=0, keepdims=True))
            alpha = jnp.exp2(m_prev - m_new)
            p = jnp.exp2(s - m_new).astype(BF16)
            acc_ref[i, :, 0:mq] = alpha * acc_ref[i, :, 0:mq] + _dot(vt_all[c, vrow:vrow + AUG, :], p)
            m_ref[i, :, 0:mq] = m_new

    scores(0, 0)

    def body(i, carry):
        c = 2 * i
        scores(c + 1, 1)
        update(c, 0)
        scores(c + 2, 0)
        update(c + 1, 1)
        return carry

    lax.fori_loop(0, n_pairs, body, 0)
    update(2 * n_pairs, 0)
    outs = []
    for i, (qt, _, _) in enumerate(streams):
        acc = acc_ref[i, :, 0:qt.shape[1]]
        outs.append(acc[0:HEAD_DIM] / acc[HEAD_DIM:HEAD_DIM + 1])
    return outs


def _place_rows(x, offset, total=LANES):
    parts = []
    if offset:
        parts.append(jnp.zeros((offset, x.shape[1]), x.dtype))
    parts.append(x)
    if total - offset - x.shape[0]:
        parts.append(jnp.zeros((total - offset - x.shape[0], x.shape[1]), x.dtype))
    return jnp.concatenate(parts, axis=0)


def _gather_kv(refs, n_lat, k_all, vt_all):
    if n_lat:
        q_ref, kl_ref, kc_ref, vl_ref, vc_ref = refs[:5]
        rest = refs[5:]
    else:
        q_ref, kc_ref, vc_ref = refs[:3]
        rest = refs[3:]

    @pl.when(pl.program_id(1) == 0)
    def _():
        if n_lat:
            k_all[0:n_lat * KC] = kl_ref[0]
            vt_all[0:n_lat] = vl_ref[0]
        k_all[n_lat * KC:(n_lat + 1) * KC] = kc_ref[0]
        vt_all[n_lat] = vc_ref[0, 0]
    return q_ref, rest


def _gqa_kernel(*refs, n_lat, Tq):
    k_all, vt_all, s_ref, m_ref, acc_ref = refs[-5:]
    q_ref, (o_ref,) = _gather_kv(refs[:-5], n_lat, k_all, vt_all)
    streams = []
    for g in range(A_KV_HEADS):
        heads = [_place_rows(q_ref[0, (g * A_REP + r) * HEAD_DIM:(g * A_REP + r + 1) * HEAD_DIM, :], g * HEAD_DIM)
                 for r in range(A_REP)]
        streams.append((jnp.concatenate(heads, axis=1), 0, g * AUG))
    outs = _flash_t(streams, k_all, vt_all, n_lat // 2, s_ref, m_ref, acc_ref)
    for g in range(A_KV_HEADS):
        for k in range(A_REP // 2):
            pair = jnp.concatenate([outs[g][:, (2 * k) * Tq:(2 * k + 1) * Tq],
                                    outs[g][:, (2 * k + 1) * Tq:(2 * k + 2) * Tq]], axis=0)
            blk = g * (A_REP // 2) + k
            o_ref[0, :, blk * LANES:(blk + 1) * LANES] = pair.T.astype(BF16)


def _diff_kernel(*refs, n_lat, Tq, lam_init):
    k_all, vt_all, s_ref, m_ref, acc_ref = refs[-5:]
    q_ref, (lam_ref, gsub_ref, o_ref) = _gather_kv(refs[:-5], n_lat, k_all, vt_all)
    lv = lam_ref[...]
    lam = (jnp.exp(jnp.sum(lv[0:1] * lv[1:2], axis=-1, keepdims=True))
           - jnp.exp(jnp.sum(lv[2:3] * lv[3:4], axis=-1, keepdims=True)) + lam_init)
    for blk in range(C_HEADS // 2):
        streams = []
        for half in range(2):
            hd = 2 * blk + half
            maps = [_place_rows(q_ref[0, hd * HEAD_DIM + mth * DIFF_DIM:hd * HEAD_DIM + (mth + 1) * DIFF_DIM, :],
                                half * HEAD_DIM + mth * DIFF_DIM) for mth in range(2)]
            streams.append((jnp.concatenate(maps, axis=1), blk * LANES, hd * AUG))
        outs = _flash_t(streams, k_all, vt_all, n_lat // 2, s_ref, m_ref, acc_ref)
        normed = []
        for o in outs:
            d = o[:, 0:Tq] - lam * o[:, Tq:2 * Tq]
            ms = jnp.mean(d * d, axis=0, keepdims=True)
            normed.append(d * lax.rsqrt(ms + EPS))
        pair = jnp.concatenate(normed, axis=0).T * (gsub_ref[...] * (1.0 - lam_init))
        o_ref[0, :, blk * LANES:(blk + 1) * LANES] = pair.astype(BF16)


def _flash_call(kernel, name, q_blk, k_blk, vt_blk, out_w, Tq, n_streams, mq, qry, lat, ctx, extra=()):
    qt = qry[1]
    b, _, sq = qt.shape
    ins = [qt]
    specs = [pl.BlockSpec((1, q_blk[0], Tq), lambda i, t: (i, q_blk[1], t))]
    k_ctx_spec = pl.BlockSpec((1, CTX_LEN, k_blk[0]), lambda i, t: (i, 0, k_blk[1]))
    v_ctx_spec = pl.BlockSpec((1, 1, vt_blk[0], KC), lambda i, t: (i, 0, vt_blk[1], 0))
    n_lat = 0
    if lat is not None:
        s_lat = lat[0].shape[1]
        n_lat = s_lat // KC
        ins += [lat[0], ctx[0], lat[3], ctx[3]]
        specs += [pl.BlockSpec((1, s_lat, k_blk[0]), lambda i, t: (i, 0, k_blk[1])), k_ctx_spec,
                  pl.BlockSpec((1, s_lat // KC, vt_blk[0], KC), lambda i, t: (i, 0, vt_blk[1], 0)), v_ctx_spec]
    else:
        ins += [ctx[0], ctx[3]]
        specs += [k_ctx_spec, v_ctx_spec]
    for arr in extra:
        ins.append(arr)
        specs.append(pl.BlockSpec(arr.shape, lambda i, t, nd=arr.ndim: (0,) * nd))
    return pl.pallas_call(
        kernel,
        out_shape=jax.ShapeDtypeStruct((b, sq, out_w), BF16),
        grid=(b, sq // Tq),
        in_specs=specs,
        out_specs=pl.BlockSpec((1, Tq, out_w), lambda i, t: (i, t, 0)),
        scratch_shapes=[pltpu.VMEM(((n_lat + 1) * KC, k_blk[0]), BF16),
                        pltpu.VMEM((n_lat + 1, vt_blk[0], KC), BF16),
                        pltpu.VMEM((2, n_streams, KC, mq), F32),
                        pltpu.VMEM((n_streams, 1, mq), F32), pltpu.VMEM((n_streams, AUG, mq), F32)],
        compiler_params=_cparams(("parallel", "arbitrary")),
        name=name,
    )(*ins)


NB_ROWS = 8
NB_Q = NB_ROWS * GRID_W
NB_KROWS = 16
NB_K = NB_KROWS * GRID_W
NB_CHUNKS = NB_K // KC


def _nbr_head(q_ref, hd):
    blk, half = hd // 2, hd % 2
    q128 = q_ref[0, :, blk * LANES:(blk + 1) * LANES]
    keep = _lane_mask(q128.shape, half * HEAD_DIM, (half + 1) * HEAD_DIM)
    return jnp.where(keep, q128, jnp.zeros_like(q128))


def _pair(even, odd):
    return jnp.where(_lane_mask(even.shape, 0, HEAD_DIM), even, odd)


def _nbr_kernel(q_ref, ktl_ref, ktc_ref, vl_ref, vc_ref, bias_ref, o_ref, s_ref):
    rb = pl.program_id(1)
    c0 = jnp.clip(2 * rb - 1, 0, SEQ // KC - NB_CHUNKS)
    tiles = []
    for hd in range(B_HEADS):
        blk, half = hd // 2, hd % 2
        rows = slice(blk * LANES, (blk + 1) * LANES)
        q = _nbr_head(q_ref, hd)
        for j in range(NB_CHUNKS):
            s_ref[:, j * KC:(j + 1) * KC] = _dot(q, ktl_ref[0, c0 + j, rows, :]) + bias_ref[0, hd, :, j * KC:(j + 1) * KC]
        s_ref[:, NB_K:] = _dot(q, ktc_ref[0, 0, rows, :])
        s = s_ref[...]
        p = jnp.exp2(s - jnp.max(s, axis=-1, keepdims=True)).astype(BF16)
        vmask = _lane_mask((KC, LANES), half * HEAD_DIM, (half + 1) * HEAD_DIM)
        vc = vc_ref[0, :, rows]
        acc = _dot(p[:, NB_K:], jnp.where(vmask, vc, jnp.ones_like(vc)))
        for j in range(NB_CHUNKS):
            row0 = pl.multiple_of((c0 + j) * KC, KC)
            v = vl_ref[0, pl.ds(row0, KC), rows]
            acc += _dot(p[:, j * KC:(j + 1) * KC], jnp.where(vmask, v, jnp.ones_like(v)))
        tiles.append(acc / pltpu.roll(acc, HEAD_DIM, 1))
        if half == 1:
            o_ref[0, :, rows] = _pair(tiles[-2], tiles[-1]).astype(BF16)


def _nbr_call(lat, ctx, bias):
    slab_l, _, kt_l, _ = lat
    slab_c, _, kt_c, _ = ctx
    b = slab_l.shape[0]
    nl = kt_l.shape[1]
    cls = lambda t: jnp.where(t == 0, 0, jnp.where(t == GRID_ROWS // NB_ROWS - 1, 2, 1))
    return pl.pallas_call(
        _nbr_kernel,
        out_shape=jax.ShapeDtypeStruct((b, SEQ, QB_W), BF16),
        grid=(b, GRID_ROWS // NB_ROWS),
        in_specs=[pl.BlockSpec((1, NB_Q, QB_W), lambda i, t: (i, t, _QB_BLK[1])),
                  pl.BlockSpec((1, nl, QB_W, KC), lambda i, t: (i, 0, 0, 0)),
                  pl.BlockSpec((1, 1, QB_W, KC), lambda i, t: (i, 0, 0, 0)),
                  pl.BlockSpec((1, SEQ, QB_W), lambda i, t: (i, 0, _VB_BLK[1])),
                  pl.BlockSpec((1, CTX_LEN, QB_W), lambda i, t: (i, 0, _VB_BLK[1])),
                  pl.BlockSpec((1, B_HEADS, NB_Q, NB_K), lambda i, t: (cls(t), 0, 0, 0))],
        out_specs=pl.BlockSpec((1, NB_Q, QB_W), lambda i, t: (i, t, 0)),
        scratch_shapes=[pltpu.VMEM((NB_Q, NB_K + CTX_LEN), F32)],
        compiler_params=_cparams(("parallel", "arbitrary")),
        name="nbr_attn",
    )(slab_l, kt_l, kt_c, slab_l, slab_c, bias)


def _plain_ctx_kernel(q_ref, ktc_ref, vc_ref, o_ref):
    tiles = []
    for hd in range(B_HEADS):
        blk, half = hd // 2, hd % 2
        rows = slice(blk * LANES, (blk + 1) * LANES)
        s = _dot(_nbr_head(q_ref, hd), ktc_ref[0, 0, rows, :])
        p = jnp.exp2(s - jnp.max(s, axis=-1, keepdims=True)).astype(BF16)
        vmask = _lane_mask((KC, LANES), half * HEAD_DIM, (half + 1) * HEAD_DIM)
        vc = vc_ref[0, :, rows]
        acc = _dot(p, jnp.where(vmask, vc, jnp.ones_like(vc)))
        tiles.append(acc / pltpu.roll(acc, HEAD_DIM, 1))
        if half == 1:
            o_ref[0, :, rows] = _pair(tiles[-2], tiles[-1]).astype(BF16)


def _plain_ctx_call(ctx):
    slab_c, _, kt_c, _ = ctx
    b = slab_c.shape[0]
    return pl.pallas_call(
        _plain_ctx_kernel,
        out_shape=jax.ShapeDtypeStruct((b, CTX_LEN, QB_W), BF16),
        grid=(b,),
        in_specs=[pl.BlockSpec((1, CTX_LEN, QB_W), lambda i: (i, 0, _QB_BLK[1])),
                  pl.BlockSpec((1, 1, QB_W, KC), lambda i: (i, 0, 0, 0)),
                  pl.BlockSpec((1, CTX_LEN, QB_W), lambda i: (i, 0, _VB_BLK[1]))],
        out_specs=pl.BlockSpec((1, CTX_LEN, QB_W), lambda i: (i, 0, 0)),
        compiler_params=_cparams(("parallel",)),
        name="plain_attn_ctx",
    )(slab_c, kt_c, slab_c)


def _nbr_bias_tables(rpb):
    n_blocks = GRID_ROWS // NB_ROWS
    sel_r = np.zeros((3, NB_ROWS, NB_KROWS, 2 * WIN_H - 1), np.float32)
    for cls, rbk in enumerate((0, 1, n_blocks - 1)):
        ws = min(max(NB_ROWS * rbk - WIN_H // 2, 0), GRID_ROWS - NB_KROWS)
        for rq in range(NB_ROWS):
            r = NB_ROWS * rbk + rq
            rs = min(max(r - WIN_H // 2, 0), GRID_ROWS - WIN_H)
            for kr in range(NB_KROWS):
                if rs <= ws + kr < rs + WIN_H:
                    sel_r[cls, rq, kr, ws + kr - r + WIN_H - 1] = 1.0
    sel_c = np.zeros((GRID_W, GRID_W, 2 * WIN_W - 1), np.float32)
    for c in range(GRID_W):
        cs = min(max(c - WIN_W // 2, 0), GRID_W - WIN_W)
        for kc in range(cs, cs + WIN_W):
            sel_c[c, kc, kc - c + WIN_W - 1] = 1.0
    valid = np.einsum("xqk,cd->xqckd", sel_r.sum(-1), sel_c.sum(-1)) > 0.5
    t = jnp.einsum("xqki,hij->xhqkj", jnp.asarray(sel_r), rpb * LOG2E, precision=lax.Precision.HIGHEST)
    t = jnp.einsum("xhqkj,cdj->xhqckd", t, jnp.asarray(sel_c), precision=lax.Precision.HIGHEST)
    t = jnp.where(jnp.asarray(valid)[:, None], t, NEG)
    return t.reshape(3, rpb.shape[0], NB_Q, NB_K)


def _outproj_kernel(x_ref, mod_ref, oa_ref, ob_ref, oc_ref, w_ref, o_ref):
    y = (_dot(oa_ref[0], w_ref[0:QA_W]) + _dot(ob_ref[0], w_ref[QA_W:QA_W + QB_W])
         + _dot(oc_ref[0], w_ref[QA_W + QB_W:]))
    o_ref[0] = x_ref[0] + mod_ref[0, 2:3, :] * y


def _outproj(x, mod_l, mod_row, oa, ob, oc, w, T):
    b, s, _ = x.shape
    tile = lambda wd: pl.BlockSpec((1, T, wd), lambda i, t: (i, t, 0))
    return pl.pallas_call(
        _outproj_kernel,
        out_shape=jax.ShapeDtypeStruct(x.shape, F32),
        grid=(b, s // T),
        in_specs=[tile(D_MODEL), pl.BlockSpec((1, 6, D_MODEL), lambda i, t: (mod_row(i), 0, 0)),
                  tile(QA_W), tile(QB_W), tile(QC_W),
                  pl.BlockSpec((D_MODEL, D_MODEL), lambda i, t: (0, 0))],
        out_specs=tile(D_MODEL),
        compiler_params=_cparams(("parallel", "parallel")),
        name="out_proj",
    )(x, mod_l, oa, ob, oc, w)


FFN_CHUNK = 256
HALO = SUBLANES


def _ffn_kernel(x_ref, xp_ref, xn_ref, mod_ref, g2_ref, wup_ref, cw_ref, cb_ref, wdn_ref, o_ref, acc_ref, *, T):
    t = pl.program_id(1)
    nt = pl.num_programs(1)
    x = x_ref[0]
    xa = jnp.concatenate([xp_ref[0], x, xn_ref[0]], axis=0)
    ms = jnp.mean(xa * xa, axis=-1, keepdims=True)
    h = xa * lax.rsqrt(ms + EPS) * g2_ref[...]
    h = h * (1.0 + mod_ref[0, 4:5, :]) + mod_ref[0, 3:4, :]
    row = lax.broadcasted_iota(jnp.int32, (T + 2 * HALO, 1), 0)
    inside = ((row >= HALO) | (t > 0)) & ((row < T + HALO) | (t < nt - 1))
    h = jnp.where(inside, h, 0.0).astype(BF16)

    def conv(y, col0):
        w = cw_ref[:, col0:col0 + FFN_CHUNK]
        return (y[HALO - 1:HALO - 1 + T] * w[0:1] + y[HALO:HALO + T] * w[1:2]
                + y[HALO + 1:HALO + 1 + T] * w[2:3] + cb_ref[:, col0:col0 + FFN_CHUNK])

    for j in range(FFN_DIM // FFN_CHUNK):
        cu, cg = j * FFN_CHUNK, FFN_DIM + j * FFN_CHUNK
        u = conv(_dot(h, wup_ref[:, cu:cu + FFN_CHUNK]), cu)
        g = conv(_dot(h, wup_ref[:, cg:cg + FFN_CHUNK]), cg)
        a = (g * (1.0 / (1.0 + jnp.exp(-g))) * u).astype(BF16)
        part = _dot(a, wdn_ref[cu:cu + FFN_CHUNK, :])
        if j == 0:
            acc_ref[...] = part
        else:
            acc_ref[...] += part
    o_ref[0] = x + mod_ref[0, 5:6, :] * acc_ref[...]


def _ffn(x, mod_l, mod_row, g2, w_up, conv_w, conv_b, w_down, T):
    b, s, _ = x.shape
    nh = T // HALO
    last = s // HALO - 1
    const = lambda shape: pl.BlockSpec(shape, lambda i, t: (0,) * len(shape), pipeline_mode=pl.Buffered(1))
    return pl.pallas_call(
        functools.partial(_ffn_kernel, T=T),
        out_shape=jax.ShapeDtypeStruct(x.shape, F32),
        grid=(b, s // T),
        in_specs=[pl.BlockSpec((1, T, D_MODEL), lambda i, t: (i, t, 0)),
                  pl.BlockSpec((1, HALO, D_MODEL), lambda i, t: (i, jnp.maximum(t * nh - 1, 0), 0)),
                  pl.BlockSpec((1, HALO, D_MODEL), lambda i, t: (i, jnp.minimum((t + 1) * nh, last), 0)),
                  pl.BlockSpec((1, 6, D_MODEL), lambda i, t: (mod_row(i), 0, 0)),
                  const((1, D_MODEL)), const((D_MODEL, 2 * FFN_DIM)), const((3, 2 * FFN_DIM)),
                  const((1, 2 * FFN_DIM)), const((FFN_DIM, D_MODEL))],
        out_specs=pl.BlockSpec((1, T, D_MODEL), lambda i, t: (i, t, 0)),
        scratch_shapes=[pltpu.VMEM((T, D_MODEL), F32)],
        compiler_params=_cparams(("parallel", "arbitrary")),
        name="conv_ffn",
    )(x, x, x, mod_l, g2, w_up, conv_w, conv_b, w_down)


def _split_pairs(w, n_groups, dim):
    lead = w.shape[:-1]
    return w.reshape(lead + (n_groups, dim // 2, 2)).swapaxes(-1, -2).reshape(lead + (n_groups * dim,))


def _rope_angles(dim):
    half = dim // 2
    inv = ROPE_THETA ** (-jnp.arange(0, half, 2, dtype=F32) / half)
    t = jnp.arange(SEQ, dtype=jnp.int32)
    row = (t // GRID_W).astype(F32)
    col = (t % GRID_W).astype(F32)
    return jnp.concatenate([row[:, None] * inv, col[:, None] * inv], axis=-1)


def _rope_tables(dim):
    half = dim // 2
    ang = _rope_angles(dim)
    cos = jnp.tile(jnp.cos(ang), (1, LANES // half))
    sin = jnp.tile(jnp.concatenate([-jnp.sin(ang), jnp.sin(ang)], axis=-1), (1, LANES // dim))
    return cos, sin, jnp.cos(ang).T, jnp.sin(ang).T


def _group_ones(group):
    idx = np.arange(LANES) // group
    return jnp.asarray((idx[:, None] == idx[None, :]).astype(np.float32), dtype=BF16)


def _layer_params(l, w_in, gq_a, gk_a, gq_b, gk_b, gq_c, gk_c, T):
    w = w_in[l]
    o = np.cumsum((0, QA_W, KA_W, KA_W, QB_W, QB_W, QB_W, QC_W, QC_W, QC_W))
    qa, ka, va, qb, kb, vb, qc, kc, vc = (w[:, o[i]:o[i + 1]] for i in range(9))
    wn = jnp.concatenate([qb, vb, _split_pairs(kc, 2 * C_HEADS, DIFF_DIM), _split_pairs(ka, A_KV_HEADS, HEAD_DIM)],
                         axis=1).astype(BF16)
    wt = jnp.concatenate([_split_pairs(qa, A_HEADS, HEAD_DIM), _split_pairs(qc, 2 * C_HEADS, DIFF_DIM), kb, va, vc],
                         axis=1).T.astype(BF16)
    gqa = _split_pairs(gq_a[l], 1, HEAD_DIM) * (HEAD_DIM ** -0.5 * LOG2E)
    gka = _split_pairs(gk_a[l], 1, HEAD_DIM)
    gqc = _split_pairs(gq_c[l], 1, DIFF_DIM) * (DIFF_DIM ** -0.5 * LOG2E)
    gkc = _split_pairs(gk_c[l], 1, DIFF_DIM)
    gn = jnp.concatenate([jnp.tile(gq_b[l] * (HEAD_DIM ** -0.5 * LOG2E), B_HEADS), jnp.ones((QB_W,), F32),
                          jnp.tile(gkc, 2 * C_HEADS), jnp.tile(gka, A_KV_HEADS)]).reshape(1, NAT_COLS)
    gt = jnp.concatenate([jnp.tile(gqa, A_HEADS), jnp.tile(gqc, 2 * C_HEADS), jnp.tile(gk_b[l], B_HEADS)])
    return wn, wt, gn, jnp.broadcast_to(gt[:, None], (QT_ROWS + QB_W, T))


def kernel(x, c, ctx, c_ctx, w_ada, b_ada, g_norm1, w_in, gq_a, gk_a, gq_b, gk_b, rpb_b, gq_c, gk_c,
           lambda_q1, lambda_k1, lambda_q2, lambda_k2, g_subln, w_out, g_norm2, w_up, conv_w, conv_b, w_down):
    depth = w_ada.shape[0]
    batch = x.shape[0]
    assert x.shape == (batch, SEQ, D_MODEL) and ctx.shape == (batch, CTX_LEN, D_MODEL) and batch <= 8

    c_all = jnp.zeros((16, D_MODEL), F32).at[:batch].set(c).at[8].set(c_ctx)
    mod = _modulation(c_all, w_ada, b_ada).reshape(depth, 16, 6, D_MODEL)
    row_x = lambda i: i
    row_c = lambda i: 8

    cos_a, sin_a, cos_at, sin_at = _rope_tables(HEAD_DIM)
    cos_c, sin_c, cos_ct, sin_ct = _rope_tables(DIFF_DIM)
    tabs_x = (cos_a, sin_a, cos_c, sin_c, cos_at, sin_at, cos_ct, sin_ct)
    ident = lambda t: jnp.ones((t.shape[0] if t.shape[0] != SEQ else CTX_LEN,
                                t.shape[1] if t.shape[1] != SEQ else CTX_LEN), F32)
    tabs_c = tuple(ident(t) if i % 2 == 0 else jnp.zeros_like(ident(t)) for i, t in enumerate(tabs_x))
    e64, e32 = _group_ones(HEAD_DIM), _group_ones(DIFF_DIM)
    T_IN = KC

    for l in range(depth):
        with_ctx = l < depth - 1
        lam_init = 0.8 - 0.6 * math.exp(-0.3 * l)
        mod_l = mod[l]
        wn, wt, gn, gt = _layer_params(l, w_in, gq_a, gk_a, gq_b, gk_b, gq_c, gk_c, T_IN)
        g1 = g_norm1[l].reshape(1, D_MODEL)
        g2 = g_norm2[l].reshape(1, D_MODEL)
        lam_vec = jnp.zeros((SUBLANES, LANES), F32).at[0:4, :DIFF_DIM].set(
            jnp.stack([lambda_q1[l], lambda_k1[l], lambda_q2[l], lambda_k2[l]]))
        gsub = jnp.tile(g_subln[l], 2).reshape(1, LANES)
        w_o = w_out[l].astype(BF16)
        ffn_w = (w_up[l].astype(BF16), conv_w[l], conv_b[l].reshape(1, -1), w_down[l].astype(BF16))

        lat = _inproj(x, mod_l, row_x, g1, wn, wt, gn, gt, e64, e32, tabs_x, T_IN)
        cx = _inproj(ctx, mod_l, row_c, g1, wn, wt, gn, gt, e64, e32, tabs_c, T_IN)

        gqa = lambda n_lat, tq: functools.partial(_gqa_kernel, n_lat=n_lat, Tq=tq)
        dif = lambda n_lat, tq: functools.partial(_diff_kernel, n_lat=n_lat, Tq=tq, lam_init=lam_init)
        oa = _flash_call(gqa(SEQ // KC, 256), "gqa_attn", _QTA_BLK, _KA_BLK, _VTA_BLK, QA_W, 256,
                         A_KV_HEADS, A_REP * 256, lat, lat, cx)
        oc = _flash_call(dif(SEQ // KC, 512), "diff_attn", _QTC_BLK, _KC_BLK, _VTC_BLK, QC_W, 512,
                         2, 2 * 512, lat, lat, cx, (lam_vec, gsub))
        ob = _nbr_call(lat, cx, _nbr_bias_tables(rpb_b[l]))
        x = _outproj(x, mod_l, row_x, oa, ob, oc, w_o, 512)
        x = _ffn(x, mod_l, row_x, g2, *ffn_w, 256)

        if with_ctx:
            ta = _flash_call(gqa(0, CTX_LEN), "gqa_attn_ctx", _QTA_BLK, _KA_BLK, _VTA_BLK, QA_W, CTX_LEN,
                             A_KV_HEADS, A_REP * CTX_LEN, cx, None, cx)
            tc = _flash_call(dif(0, CTX_LEN), "diff_attn_ctx", _QTC_BLK, _KC_BLK, _VTC_BLK, QC_W, CTX_LEN,
                             2, 2 * CTX_LEN, cx, None, cx, (lam_vec, gsub))
            tb = _plain_ctx_call(cx)
            ctx = _outproj(ctx, mod_l, row_c, ta, tb, tc, w_o, CTX_LEN)
            ctx = _ffn(ctx, mod_l, row_c, g2, *ffn_w, CTX_LEN)
    return x
```

```python
import functools
import math

import numpy as np
import jax
import jax.numpy as jnp
from jax import lax
from jax.experimental import pallas as pl
from jax.experimental.pallas import tpu as pltpu

F32 = jnp.float32
BF16 = jnp.bfloat16

D_MODEL = 1024
SEQ = 4096
GRID_W = 64
GRID_ROWS = SEQ // GRID_W
CTX_LEN = 256
HEAD_DIM = 64
A_HEADS = 8
A_KV_HEADS = 2
A_REP = A_HEADS // A_KV_HEADS
B_HEADS = 4
C_HEADS = 4
DIFF_DIM = 32
WIN_H = 8
WIN_W = 16
FFN_DIM = 2816
ROPE_THETA = 10000.0
EPS = 1e-6
NEG = -1e30
LOG2E = 1.4426950408889634

LANES = 128
SUBLANES = 8
KC = 256
QTILE = 256
FLASH_UNROLL = 4
VMEM_LIMIT = 56 * 1024 * 1024

QA_W, QB_W, QC_W = A_HEADS * HEAD_DIM, B_HEADS * HEAD_DIM, C_HEADS * HEAD_DIM
KA_W = A_KV_HEADS * HEAD_DIM
NAT_COLS = QB_W + QB_W + QC_W + KA_W
_QB_BLK, _VB_BLK, _KC_BLK, _KA_BLK = (QB_W, 0), (QB_W, 1), (QC_W, 2), (KA_W, 6)
TR_ROWS = QA_W + QC_W + QB_W + KA_W + QC_W
QT_ROWS = QA_W + QC_W
_QTA_BLK, _QTC_BLK = (QA_W, 0), (QC_W, 2)
ONES_ROWS = 16
AUG = HEAD_DIM + ONES_ROWS
VT_ROWS = C_HEADS * AUG + A_KV_HEADS * AUG
_VTC_BLK, _VTA_BLK = (C_HEADS * AUG, 0), (A_KV_HEADS * AUG, 2)


def _cparams(sem):
    return pltpu.CompilerParams(dimension_semantics=sem, vmem_limit_bytes=VMEM_LIMIT)


def _dot(a, b):
    return jnp.dot(a, b, preferred_element_type=F32)


def _lane_mask(shape, lo, hi):
    lane = lax.broadcasted_iota(jnp.int32, shape, 1)
    return (lane >= lo) & (lane < hi)


def _mod_kernel(c_ref, w_ref, b_ref, o_ref):
    c = c_ref[...]
    a = c * (1.0 / (1.0 + jnp.exp(-c)))
    a_hi = a.astype(BF16)
    a_lo = (a - a_hi.astype(F32)).astype(BF16)
    w = w_ref[0]
    w_hi = w.astype(BF16)
    w_lo = (w - w_hi.astype(F32)).astype(BF16)
    o_ref[0] = _dot(a_hi, w_hi) + _dot(a_hi, w_lo) + _dot(a_lo, w_hi) + b_ref[0]


def _modulation(c_all, w_ada, b_ada):
    depth = w_ada.shape[0]
    n = w_ada.shape[2]
    nb = 1536
    return pl.pallas_call(
        _mod_kernel,
        out_shape=jax.ShapeDtypeStruct((depth, 16, n), F32),
        grid=(depth, n // nb),
        in_specs=[pl.BlockSpec((16, D_MODEL), lambda l, j: (0, 0)),
                  pl.BlockSpec((1, D_MODEL, nb), lambda l, j: (l, 0, j)),
                  pl.BlockSpec((1, 1, nb), lambda l, j: (l, 0, j))],
        out_specs=pl.BlockSpec((1, 16, nb), lambda l, j: (l, 0, j)),
        compiler_params=_cparams(("arbitrary", "arbitrary")),
        name="adaln_mod",
    )(c_all, w_ada, b_ada.reshape(depth, 1, n))


_NAT_CHUNKS = [(64, None)] * 2 + [(None, None)] * 2 + [(32, "c")] * 2 + [(64, "a")]


def _swap_halves(n, group):
    half = group // 2
    lane = lax.broadcasted_iota(jnp.int32, n.shape, 1)
    first = (lane % group) < half
    return jnp.where(first, pltpu.roll(n, LANES - half, 1), pltpu.roll(n, half, 1))


def _norm_rope_t(a, gain, cos, sin):
    d = a.shape[0]
    ss = jnp.sum(a * a, axis=0, keepdims=True)
    n = a * lax.rsqrt(ss * (1.0 / d) + EPS) * gain
    if cos is None:
        return n
    x1, x2 = n[0:d // 2], n[d // 2:d]
    return jnp.concatenate([x1 * cos - x2 * sin, x1 * sin + x2 * cos], axis=0)


def _inproj_kernel(x_ref, mod_ref, g1_ref, wn_ref, wt_ref, gn_ref, gt_ref, e64_ref, e32_ref,
                   cosa_ref, sina_ref, cosc_ref, sinc_ref, cosat_ref, sinat_ref, cosct_ref, sinct_ref,
                   slab_ref, qt_ref, ktb_ref, vt_ref, *, T):
    x = x_ref[0]
    ms = jnp.mean(x * x, axis=-1, keepdims=True)
    xn = x * lax.rsqrt(ms + EPS) * g1_ref[...]
    h = (xn * (1.0 + mod_ref[0, 1:2, :]) + mod_ref[0, 0:1, :]).astype(BF16)

    nat = _dot(h, wn_ref[...])
    for ci, (group, rope) in enumerate(_NAT_CHUNKS):
        cols = slice(ci * LANES, (ci + 1) * LANES)
        a = nat[:, cols]
        if group is not None:
            y = a * a
            y_hi = y.astype(BF16)
            y_lo = (y - y_hi.astype(F32)).astype(BF16)
            e = e64_ref[...] if group == 64 else e32_ref[...]
            ss = _dot(y_hi, e) + _dot(y_lo, e)
            a = a * lax.rsqrt(ss * (1.0 / group) + EPS) * gn_ref[:, cols]
        if rope == "a":
            a = a * cosa_ref[...] + _swap_halves(a, 64) * sina_ref[...]
        elif rope == "c":
            a = a * cosc_ref[...] + _swap_halves(a, 32) * sinc_ref[...]
        slab_ref[0, :, cols] = a.astype(BF16)

    tr = lax.dot_general(wt_ref[...], h, (((1,), (1,)), ((), ())), preferred_element_type=F32)
    for hd in range(A_HEADS):
        rows = slice(hd * HEAD_DIM, (hd + 1) * HEAD_DIM)
        qt_ref[0, rows, :] = _norm_rope_t(tr[rows], gt_ref[rows, :], cosat_ref[...], sinat_ref[...]).astype(BF16)
    for sh in range(2 * C_HEADS):
        rows = slice(QA_W + sh * DIFF_DIM, QA_W + (sh + 1) * DIFF_DIM)
        qt_ref[0, rows, :] = _norm_rope_t(tr[rows], gt_ref[rows, :], cosct_ref[...], sinct_ref[...]).astype(BF16)
    ones = jnp.ones((ONES_ROWS, KC), BF16)
    for j in range(T // KC):
        tok = slice(j * KC, (j + 1) * KC)
        for hd in range(B_HEADS):
            rows = slice(QT_ROWS + hd * HEAD_DIM, QT_ROWS + (hd + 1) * HEAD_DIM)
            ktb_ref[0, j, hd * HEAD_DIM:(hd + 1) * HEAD_DIM, :] = _norm_rope_t(
                tr[rows, tok], gt_ref[rows, tok], None, None).astype(BF16)
        v0 = QT_ROWS + QB_W
        for i in range(A_KV_HEADS + C_HEADS):
            src = v0 + i * HEAD_DIM
            dst = (C_HEADS + i if i < A_KV_HEADS else i - A_KV_HEADS) * AUG
            vt_ref[0, j, dst:dst + HEAD_DIM, :] = tr[src:src + HEAD_DIM, tok].astype(BF16)
            vt_ref[0, j, dst + HEAD_DIM:dst + AUG, :] = ones


def _inproj(x, mod_l, mod_row, g1, wn, wt, gn, gt, e64, e32, tabs, T):
    b, s, _ = x.shape
    full = lambda shape: pl.BlockSpec(shape, lambda i, t: (0,) * len(shape))
    tab = pl.BlockSpec((T, LANES), lambda i, t: (t, 0))
    tab_t = lambda r: pl.BlockSpec((r, T), lambda i, t: (0, t))
    return pl.pallas_call(
        functools.partial(_inproj_kernel, T=T),
        out_shape=(jax.ShapeDtypeStruct((b, s, NAT_COLS), BF16),
                   jax.ShapeDtypeStruct((b, QT_ROWS, s), BF16),
                   jax.ShapeDtypeStruct((b, s // KC, QB_W, KC), BF16),
                   jax.ShapeDtypeStruct((b, s // KC, VT_ROWS, KC), BF16)),
        grid=(b, s // T),
        in_specs=[pl.BlockSpec((1, T, D_MODEL), lambda i, t: (i, t, 0)),
                  pl.BlockSpec((1, 6, D_MODEL), lambda i, t: (mod_row(i), 0, 0)),
                  full((1, D_MODEL)), full((D_MODEL, NAT_COLS)), full((TR_ROWS, D_MODEL)),
                  full((1, NAT_COLS)), full((QT_ROWS + QB_W, T)), full((LANES, LANES)), full((LANES, LANES)),
                  tab, tab, tab, tab,
                  tab_t(HEAD_DIM // 2), tab_t(HEAD_DIM // 2), tab_t(DIFF_DIM // 2), tab_t(DIFF_DIM // 2)],
        out_specs=(pl.BlockSpec((1, T, NAT_COLS), lambda i, t: (i, t, 0)),
                   pl.BlockSpec((1, QT_ROWS, T), lambda i, t: (i, 0, t)),
                   pl.BlockSpec((1, T // KC, QB_W, KC), lambda i, t: (i, t, 0, 0)),
                   pl.BlockSpec((1, T // KC, VT_ROWS, KC), lambda i, t: (i, t, 0, 0))),
        compiler_params=_cparams(("parallel", "parallel")),
        name="qkv_proj",
    )(x, mod_l, g1, wn, wt, gn, gt, e64, e32, *tabs)


def _flash_t(streams, k_all, vt_all, n_pairs, s_ref, m_ref, acc_ref):
    for i, (qt, _, _) in enumerate(streams):
        mq = qt.shape[1]
        m_ref[i, :, 0:mq] = jnp.full((1, mq), NEG, F32)
        acc_ref[i, :, 0:mq] = jnp.zeros((AUG, mq), F32)

    tiles = [(i, slice(j, j + QTILE)) for i, (qt, _, _) in enumerate(streams) for j in range(0, qt.shape[1], QTILE)]

    def scores(keys, k, buf):
        i, cols = tiles[k]
        s_ref[buf, i, :, cols] = _dot(keys[i], streams[i][0][:, cols])

    def load_keys(c):
        row0 = pl.multiple_of(c * KC, KC)
        return [k_all[pl.ds(row0, KC), kcol:kcol + LANES] for _, kcol, _ in streams]

    def update(c, k, buf):
        i, cols = tiles[k]
        vrow = streams[i][2]
        s = s_ref[buf, i, :, cols]
        m_prev = m_ref[i, :, cols]
        m_new = jnp.maximum(m_prev, jnp.max(s, axis=0, keepdims=True))
        alpha = jnp.exp2(m_prev - m_new)
        p = jnp.exp2((s - m_new).astype(BF16))
        acc_ref[i, :, cols] = alpha * acc_ref[i, :, cols] + _dot(vt_all[c, vrow:vrow + AUG, :], p)
        m_ref[i, :, cols] = m_new

    def chunk(c, buf, last=False):
        keys = None if last else load_keys(c + 1)
        for k in range(len(tiles)):
            if not last:
                scores(keys, k, 1 - buf)
            update(c, k, buf)

    keys0 = load_keys(0)
    for k in range(len(tiles)):
        scores(keys0, k, 0)

    def body(i, carry):
        for u in range(FLASH_UNROLL):
            chunk(FLASH_UNROLL * i + u, u % 2)
        return carry

    n_chunks = 2 * n_pairs
    assert n_chunks % FLASH_UNROLL == 0 and FLASH_UNROLL % 2 == 0
    lax.fori_loop(0, n_chunks // FLASH_UNROLL, body, 0)
    chunk(n_chunks, 0, last=True)
    outs = []
    for i, (qt, _, _) in enumerate(streams):
        acc = acc_ref[i, :, 0:qt.shape[1]]
        outs.append(acc[0:HEAD_DIM] / acc[HEAD_DIM:HEAD_DIM + 1])
    return outs


def _place_rows(x, offset, total=LANES):
    parts = []
    if offset:
        parts.append(jnp.zeros((offset, x.shape[1]), x.dtype))
    parts.append(x)
    if total - offset - x.shape[0]:
        parts.append(jnp.zeros((total - offset - x.shape[0], x.shape[1]), x.dtype))
    return jnp.concatenate(parts, axis=0)


def _gather_kv(refs, n_lat, k_all, vt_all):
    if n_lat:
        q_ref, kl_ref, kc_ref, vl_ref, vc_ref = refs[:5]
        rest = refs[5:]
    else:
        q_ref, kc_ref, vc_ref = refs[:3]
        rest = refs[3:]

    @pl.when(pl.program_id(1) == 0)
    def _():
        if n_lat:
            k_all[0:n_lat * KC] = kl_ref[0]
            vt_all[0:n_lat] = vl_ref[0]
        k_all[n_lat * KC:(n_lat + 1) * KC] = kc_ref[0]
        vt_all[n_lat] = vc_ref[0, 0]
    return q_ref, rest


def _gqa_kernel(*refs, n_lat, Tq):
    k_all, vt_all, s_ref, m_ref, acc_ref = refs[-5:]
    q_ref, (o_ref,) = _gather_kv(refs[:-5], n_lat, k_all, vt_all)
    streams = []
    for g in range(A_KV_HEADS):
        heads = [_place_rows(q_ref[0, (g * A_REP + r) * HEAD_DIM:(g * A_REP + r + 1) * HEAD_DIM, :], g * HEAD_DIM)
                 for r in range(A_REP)]
        streams.append((jnp.concatenate(heads, axis=1), 0, g * AUG))
    outs = _flash_t(streams, k_all, vt_all, n_lat // 2, s_ref, m_ref, acc_ref)
    for g in range(A_KV_HEADS):
        for k in range(A_REP // 2):
            pair = jnp.concatenate([outs[g][:, (2 * k) * Tq:(2 * k + 1) * Tq],
                                    outs[g][:, (2 * k + 1) * Tq:(2 * k + 2) * Tq]], axis=0)
            blk = g * (A_REP // 2) + k
            o_ref[0, :, blk * LANES:(blk + 1) * LANES] = pair.T.astype(BF16)


def _diff_kernel(*refs, n_lat, Tq, lam_init):
    k_all, vt_all, s_ref, m_ref, acc_ref = refs[-5:]
    q_ref, (lam_ref, gsub_ref, o_ref) = _gather_kv(refs[:-5], n_lat, k_all, vt_all)
    lv = lam_ref[...]
    lam = (jnp.exp(jnp.sum(lv[0:1] * lv[1:2], axis=-1, keepdims=True))
           - jnp.exp(jnp.sum(lv[2:3] * lv[3:4], axis=-1, keepdims=True)) + lam_init)
    for blk in range(C_HEADS // 2):
        streams = []
        for half in range(2):
            hd = 2 * blk + half
            maps = [_place_rows(q_ref[0, hd * HEAD_DIM + mth * DIFF_DIM:hd * HEAD_DIM + (mth + 1) * DIFF_DIM, :],
                                half * HEAD_DIM + mth * DIFF_DIM) for mth in range(2)]
            streams.append((jnp.concatenate(maps, axis=1), blk * LANES, hd * AUG))
        outs = _flash_t(streams, k_all, vt_all, n_lat // 2, s_ref, m_ref, acc_ref)
        normed = []
        for o in outs:
            d = o[:, 0:Tq] - lam * o[:, Tq:2 * Tq]
            ms = jnp.mean(d * d, axis=0, keepdims=True)
            normed.append(d * lax.rsqrt(ms + EPS))
        pair = jnp.concatenate(normed, axis=0).T * (gsub_ref[...] * (1.0 - lam_init))
        o_ref[0, :, blk * LANES:(blk + 1) * LANES] = pair.astype(BF16)


def _flash_call(kernel, name, q_blk, k_blk, vt_blk, out_w, Tq, n_streams, mq, qry, lat, ctx, extra=()):
    qt = qry[1]
    b, _, sq = qt.shape
    ins = [qt]
    specs = [pl.BlockSpec((1, q_blk[0], Tq), lambda i, t: (i, q_blk[1], t))]
    k_ctx_spec = pl.BlockSpec((1, CTX_LEN, k_blk[0]), lambda i, t: (i, 0, k_blk[1]))
    v_ctx_spec = pl.BlockSpec((1, 1, vt_blk[0], KC), lambda i, t: (i, 0, vt_blk[1], 0))
    n_lat = 0
    if lat is not None:
        s_lat = lat[0].shape[1]
        n_lat = s_lat // KC
        ins += [lat[0], ctx[0], lat[3], ctx[3]]
        specs += [pl.BlockSpec((1, s_lat, k_blk[0]), lambda i, t: (i, 0, k_blk[1])), k_ctx_spec,
                  pl.BlockSpec((1, s_lat // KC, vt_blk[0], KC), lambda i, t: (i, 0, vt_blk[1], 0)), v_ctx_spec]
    else:
        ins += [ctx[0], ctx[3]]
        specs += [k_ctx_spec, v_ctx_spec]
    for arr in extra:
        ins.append(arr)
        specs.append(pl.BlockSpec(arr.shape, lambda i, t, nd=arr.ndim: (0,) * nd))
    return pl.pallas_call(
        kernel,
        out_shape=jax.ShapeDtypeStruct((b, sq, out_w), BF16),
        grid=(b, sq // Tq),
        in_specs=specs,
        out_specs=pl.BlockSpec((1, Tq, out_w), lambda i, t: (i, t, 0)),
        scratch_shapes=[pltpu.VMEM(((n_lat + 1) * KC, k_blk[0]), BF16),
                        pltpu.VMEM((n_lat + 1, vt_blk[0], KC), BF16),
                        pltpu.VMEM((2, n_streams, KC, mq), F32),
                        pltpu.VMEM((n_streams, 1, mq), F32), pltpu.VMEM((n_streams, AUG, mq), F32)],
        compiler_params=_cparams(("parallel", "arbitrary")),
        name=name,
    )(*ins)


NB_ROWS = 8
NB_Q = NB_ROWS * GRID_W
NB_KROWS = 16
NB_K = NB_KROWS * GRID_W
NB_CHUNKS = NB_K // KC


def _nbr_head(q_ref, hd):
    blk, half = hd // 2, hd % 2
    q128 = q_ref[0, :, blk * LANES:(blk + 1) * LANES]
    keep = _lane_mask(q128.shape, half * HEAD_DIM, (half + 1) * HEAD_DIM)
    return jnp.where(keep, q128, jnp.zeros_like(q128))


def _pair(even, odd):
    return jnp.where(_lane_mask(even.shape, 0, HEAD_DIM), even, odd)


def _nbr_kernel(q_ref, ktl_ref, ktc_ref, vl_ref, vc_ref, bias_ref, o_ref, s_ref):
    rb = pl.program_id(1)
    c0 = jnp.clip(2 * rb - 1, 0, SEQ // KC - NB_CHUNKS)
    tiles = []
    for hd in range(B_HEADS):
        blk, half = hd // 2, hd % 2
        rows = slice(blk * LANES, (blk + 1) * LANES)
        q = _nbr_head(q_ref, hd)
        for j in range(NB_CHUNKS):
            s_ref[:, j * KC:(j + 1) * KC] = _dot(q, ktl_ref[0, c0 + j, rows, :]) + bias_ref[0, hd, :, j * KC:(j + 1) * KC]
        s_ref[:, NB_K:] = _dot(q, ktc_ref[0, 0, rows, :])
        s = s_ref[...]
        p = jnp.exp2(s - jnp.max(s, axis=-1, keepdims=True)).astype(BF16)
        vmask = _lane_mask((KC, LANES), half * HEAD_DIM, (half + 1) * HEAD_DIM)
        vc = vc_ref[0, :, rows]
        acc = _dot(p[:, NB_K:], jnp.where(vmask, vc, jnp.ones_like(vc)))
        for j in range(NB_CHUNKS):
            row0 = pl.multiple_of((c0 + j) * KC, KC)
            v = vl_ref[0, pl.ds(row0, KC), rows]
            acc += _dot(p[:, j * KC:(j + 1) * KC], jnp.where(vmask, v, jnp.ones_like(v)))
        tiles.append(acc / pltpu.roll(acc, HEAD_DIM, 1))
        if half == 1:
            o_ref[0, :, rows] = _pair(tiles[-2], tiles[-1]).astype(BF16)


def _nbr_call(lat, ctx, bias):
    slab_l, _, kt_l, _ = lat
    slab_c, _, kt_c, _ = ctx
    b = slab_l.shape[0]
    nl = kt_l.shape[1]
    cls = lambda t: jnp.where(t == 0, 0, jnp.where(t == GRID_ROWS // NB_ROWS - 1, 2, 1))
    return pl.pallas_call(
        _nbr_kernel,
        out_shape=jax.ShapeDtypeStruct((b, SEQ, QB_W), BF16),
        grid=(b, GRID_ROWS // NB_ROWS),
        in_specs=[pl.BlockSpec((1, NB_Q, QB_W), lambda i, t: (i, t, _QB_BLK[1])),
                  pl.BlockSpec((1, nl, QB_W, KC), lambda i, t: (i, 0, 0, 0)),
                  pl.BlockSpec((1, 1, QB_W, KC), lambda i, t: (i, 0, 0, 0)),
                  pl.BlockSpec((1, SEQ, QB_W), lambda i, t: (i, 0, _VB_BLK[1])),
                  pl.BlockSpec((1, CTX_LEN, QB_W), lambda i, t: (i, 0, _VB_BLK[1])),
                  pl.BlockSpec((1, B_HEADS, NB_Q, NB_K), lambda i, t: (cls(t), 0, 0, 0))],
        out_specs=pl.BlockSpec((1, NB_Q, QB_W), lambda i, t: (i, t, 0)),
        scratch_shapes=[pltpu.VMEM((NB_Q, NB_K + CTX_LEN), F32)],
        compiler_params=_cparams(("parallel", "arbitrary")),
        name="nbr_attn",
    )(slab_l, kt_l, kt_c, slab_l, slab_c, bias)


def _plain_ctx_kernel(q_ref, ktc_ref, vc_ref, o_ref):
    tiles = []
    for hd in range(B_HEADS):
        blk, half = hd // 2, hd % 2
        rows = slice(blk * LANES, (blk + 1) * LANES)
        s = _dot(_nbr_head(q_ref, hd), ktc_ref[0, 0, rows, :])
        p = jnp.exp2(s - jnp.max(s, axis=-1, keepdims=True)).astype(BF16)
        vmask = _lane_mask((KC, LANES), half * HEAD_DIM, (half + 1) * HEAD_DIM)
        vc = vc_ref[0, :, rows]
        acc = _dot(p, jnp.where(vmask, vc, jnp.ones_like(vc)))
        tiles.append(acc / pltpu.roll(acc, HEAD_DIM, 1))
        if half == 1:
            o_ref[0, :, rows] = _pair(tiles[-2], tiles[-1]).astype(BF16)


def _plain_ctx_call(ctx):
    slab_c, _, kt_c, _ = ctx
    b = slab_c.shape[0]
    return pl.pallas_call(
        _plain_ctx_kernel,
        out_shape=jax.ShapeDtypeStruct((b, CTX_LEN, QB_W), BF16),
        grid=(b,),
        in_specs=[pl.BlockSpec((1, CTX_LEN, QB_W), lambda i: (i, 0, _QB_BLK[1])),
                  pl.BlockSpec((1, 1, QB_W, KC), lambda i: (i, 0, 0, 0)),
                  pl.BlockSpec((1, CTX_LEN, QB_W), lambda i: (i, 0, _VB_BLK[1]))],
        out_specs=pl.BlockSpec((1, CTX_LEN, QB_W), lambda i: (i, 0, 0)),
        compiler_params=_cparams(("parallel",)),
        name="plain_attn_ctx",
    )(slab_c, kt_c, slab_c)


def _nbr_bias_tables(rpb):
    n_blocks = GRID_ROWS // NB_ROWS
    sel_r = np.zeros((3, NB_ROWS, NB_KROWS, 2 * WIN_H - 1), np.float32)
    for cls, rbk in enumerate((0, 1, n_blocks - 1)):
        ws = min(max(NB_ROWS * rbk - WIN_H // 2, 0), GRID_ROWS - NB_KROWS)
        for rq in range(NB_ROWS):
            r = NB_ROWS * rbk + rq
            rs = min(max(r - WIN_H // 2, 0), GRID_ROWS - WIN_H)
            for kr in range(NB_KROWS):
                if rs <= ws + kr < rs + WIN_H:
                    sel_r[cls, rq, kr, ws + kr - r + WIN_H - 1] = 1.0
    sel_c = np.zeros((GRID_W, GRID_W, 2 * WIN_W - 1), np.float32)
    for c in range(GRID_W):
        cs = min(max(c - WIN_W // 2, 0), GRID_W - WIN_W)
        for kc in range(cs, cs + WIN_W):
            sel_c[c, kc, kc - c + WIN_W - 1] = 1.0
    valid = np.einsum("xqk,cd->xqckd", sel_r.sum(-1), sel_c.sum(-1)) > 0.5
    t = jnp.einsum("xqki,hij->xhqkj", jnp.asarray(sel_r), rpb * LOG2E, precision=lax.Precision.HIGHEST)
    t = jnp.einsum("xhqkj,cdj->xhqckd", t, jnp.asarray(sel_c), precision=lax.Precision.HIGHEST)
    t = jnp.where(jnp.asarray(valid)[:, None], t, NEG)
    return t.reshape(3, rpb.shape[0], NB_Q, NB_K)


def _outproj_kernel(x_ref, mod_ref, oa_ref, ob_ref, oc_ref, w_ref, o_ref):
    y = (_dot(oa_ref[0], w_ref[0:QA_W]) + _dot(ob_ref[0], w_ref[QA_W:QA_W + QB_W])
         + _dot(oc_ref[0], w_ref[QA_W + QB_W:]))
    o_ref[0] = x_ref[0] + mod_ref[0, 2:3, :] * y


def _outproj(x, mod_l, mod_row, oa, ob, oc, w, T):
    b, s, _ = x.shape
    tile = lambda wd: pl.BlockSpec((1, T, wd), lambda i, t: (i, t, 0))
    return pl.pallas_call(
        _outproj_kernel,
        out_shape=jax.ShapeDtypeStruct(x.shape, F32),
        grid=(b, s // T),
        in_specs=[tile(D_MODEL), pl.BlockSpec((1, 6, D_MODEL), lambda i, t: (mod_row(i), 0, 0)),
                  tile(QA_W), tile(QB_W), tile(QC_W),
                  pl.BlockSpec((D_MODEL, D_MODEL), lambda i, t: (0, 0))],
        out_specs=tile(D_MODEL),
        compiler_params=_cparams(("parallel", "parallel")),
        name="out_proj",
    )(x, mod_l, oa, ob, oc, w)


FFN_CHUNK = 256
HALO = SUBLANES


def _ffn_kernel(x_ref, xp_ref, xn_ref, mod_ref, g2_ref, wup_ref, cw_ref, cb_ref, wdn_ref, o_ref, y_ref, a_ref, *, T):
    t = pl.program_id(1)
    nt = pl.num_programs(1)
    x = x_ref[0]
    xa = jnp.concatenate([xp_ref[0], x, xn_ref[0]], axis=0)
    ms = jnp.mean(xa * xa, axis=-1, keepdims=True)
    h = xa * lax.rsqrt(ms + EPS) * g2_ref[...]
    h = h * (1.0 + mod_ref[0, 4:5, :]) + mod_ref[0, 3:4, :]
    row = lax.broadcasted_iota(jnp.int32, (T + 2 * HALO, 1), 0)
    inside = ((row >= HALO) | (t > 0)) & ((row < T + HALO) | (t < nt - 1))
    h = jnp.where(inside, h, 0.0).astype(BF16)

    def conv(slot, col0):
        w = cw_ref[:, col0:col0 + FFN_CHUNK]
        return (y_ref[slot, HALO - 1:HALO - 1 + T] * w[0:1] + y_ref[slot, HALO:HALO + T] * w[1:2]
                + y_ref[slot, HALO + 1:HALO + 1 + T] * w[2:3] + cb_ref[:, col0:col0 + FFN_CHUNK])

    for j in range(FFN_DIM // FFN_CHUNK):
        cu, cg = j * FFN_CHUNK, FFN_DIM + j * FFN_CHUNK
        su, sg = 2 * (j % 2), 2 * (j % 2) + 1
        y_ref[su] = _dot(h, wup_ref[:, cu:cu + FFN_CHUNK])
        y_ref[sg] = _dot(h, wup_ref[:, cg:cg + FFN_CHUNK])
        g = conv(sg, cg)
        a_ref[:, cu:cu + FFN_CHUNK] = (g * (1.0 / (1.0 + jnp.exp(-g))) * conv(su, cu)).astype(BF16)
    o_ref[0] = x + mod_ref[0, 5:6, :] * _dot(a_ref[...], wdn_ref[...])


def _ffn(x, mod_l, mod_row, g2, w_up, conv_w, conv_b, w_down, T):
    b, s, _ = x.shape
    nh = T // HALO
    last = s // HALO - 1
    const = lambda shape: pl.BlockSpec(shape, lambda i, t: (0,) * len(shape), pipeline_mode=pl.Buffered(1))
    return pl.pallas_call(
        functools.partial(_ffn_kernel, T=T),
        out_shape=jax.ShapeDtypeStruct(x.shape, F32),
        grid=(b, s // T),
        in_specs=[pl.BlockSpec((1, T, D_MODEL), lambda i, t: (i, t, 0)),
                  pl.BlockSpec((1, HALO, D_MODEL), lambda i, t: (i, jnp.maximum(t * nh - 1, 0), 0)),
                  pl.BlockSpec((1, HALO, D_MODEL), lambda i, t: (i, jnp.minimum((t + 1) * nh, last), 0)),
                  pl.BlockSpec((1, 6, D_MODEL), lambda i, t: (mod_row(i), 0, 0)),
                  const((1, D_MODEL)), const((D_MODEL, 2 * FFN_DIM)), const((3, 2 * FFN_DIM)),
                  const((1, 2 * FFN_DIM)), const((FFN_DIM, D_MODEL))],
        out_specs=pl.BlockSpec((1, T, D_MODEL), lambda i, t: (i, t, 0)),
        scratch_shapes=[pltpu.VMEM((4, T + 2 * HALO, FFN_CHUNK), F32), pltpu.VMEM((T, FFN_DIM), BF16)],
        compiler_params=_cparams(("parallel", "arbitrary")),
        name="conv_ffn",
    )(x, x, x, mod_l, g2, w_up, conv_w, conv_b, w_down)


def _split_pairs(w, n_groups, dim):
    lead = w.shape[:-1]
    return w.reshape(lead + (n_groups, dim // 2, 2)).swapaxes(-1, -2).reshape(lead + (n_groups * dim,))


def _rope_angles(dim):
    half = dim // 2
    inv = ROPE_THETA ** (-jnp.arange(0, half, 2, dtype=F32) / half)
    t = jnp.arange(SEQ, dtype=jnp.int32)
    row = (t // GRID_W).astype(F32)
    col = (t % GRID_W).astype(F32)
    return jnp.concatenate([row[:, None] * inv, col[:, None] * inv], axis=-1)


def _rope_tables(dim):
    half = dim // 2
    ang = _rope_angles(dim)
    cos = jnp.tile(jnp.cos(ang), (1, LANES // half))
    sin = jnp.tile(jnp.concatenate([-jnp.sin(ang), jnp.sin(ang)], axis=-1), (1, LANES // dim))
    return cos, sin, jnp.cos(ang).T, jnp.sin(ang).T


def _group_ones(group):
    idx = np.arange(LANES) // group
    return jnp.asarray((idx[:, None] == idx[None, :]).astype(np.float32), dtype=BF16)


def _layer_params(l, w_in, gq_a, gk_a, gq_b, gk_b, gq_c, gk_c, T):
    w = w_in[l]
    o = np.cumsum((0, QA_W, KA_W, KA_W, QB_W, QB_W, QB_W, QC_W, QC_W, QC_W))
    qa, ka, va, qb, kb, vb, qc, kc, vc = (w[:, o[i]:o[i + 1]] for i in range(9))
    wn = jnp.concatenate([qb, vb, _split_pairs(kc, 2 * C_HEADS, DIFF_DIM), _split_pairs(ka, A_KV_HEADS, HEAD_DIM)],
                         axis=1).astype(BF16)
    wt = jnp.concatenate([_split_pairs(qa, A_HEADS, HEAD_DIM), _split_pairs(qc, 2 * C_HEADS, DIFF_DIM), kb, va, vc],
                         axis=1).T.astype(BF16)
    gqa = _split_pairs(gq_a[l], 1, HEAD_DIM) * (HEAD_DIM ** -0.5 * LOG2E)
    gka = _split_pairs(gk_a[l], 1, HEAD_DIM)
    gqc = _split_pairs(gq_c[l], 1, DIFF_DIM) * (DIFF_DIM ** -0.5 * LOG2E)
    gkc = _split_pairs(gk_c[l], 1, DIFF_DIM)
    gn = jnp.concatenate([jnp.tile(gq_b[l] * (HEAD_DIM ** -0.5 * LOG2E), B_HEADS), jnp.ones((QB_W,), F32),
                          jnp.tile(gkc, 2 * C_HEADS), jnp.tile(gka, A_KV_HEADS)]).reshape(1, NAT_COLS)
    gt = jnp.concatenate([jnp.tile(gqa, A_HEADS), jnp.tile(gqc, 2 * C_HEADS), jnp.tile(gk_b[l], B_HEADS)])
    return wn, wt, gn, jnp.broadcast_to(gt[:, None], (QT_ROWS + QB_W, T))


def kernel(x, c, ctx, c_ctx, w_ada, b_ada, g_norm1, w_in, gq_a, gk_a, gq_b, gk_b, rpb_b, gq_c, gk_c,
           lambda_q1, lambda_k1, lambda_q2, lambda_k2, g_subln, w_out, g_norm2, w_up, conv_w, conv_b, w_down):
    depth = w_ada.shape[0]
    batch = x.shape[0]
    assert x.shape == (batch, SEQ, D_MODEL) and ctx.shape == (batch, CTX_LEN, D_MODEL) and batch <= 8

    c_all = jnp.zeros((16, D_MODEL), F32).at[:batch].set(c).at[8].set(c_ctx)
    mod = _modulation(c_all, w_ada, b_ada).reshape(depth, 16, 6, D_MODEL)
    row_x = lambda i: i
    row_c = lambda i: 8

    cos_a, sin_a, cos_at, sin_at = _rope_tables(HEAD_DIM)
    cos_c, sin_c, cos_ct, sin_ct = _rope_tables(DIFF_DIM)
    tabs_x = (cos_a, sin_a, cos_c, sin_c, cos_at, sin_at, cos_ct, sin_ct)
    ident = lambda t: jnp.ones((t.shape[0] if t.shape[0] != SEQ else CTX_LEN,
                                t.shape[1] if t.shape[1] != SEQ else CTX_LEN), F32)
    tabs_c = tuple(ident(t) if i % 2 == 0 else jnp.zeros_like(ident(t)) for i, t in enumerate(tabs_x))
    e64, e32 = _group_ones(HEAD_DIM), _group_ones(DIFF_DIM)
    T_IN = KC

    for l in range(depth):
        with_ctx = l < depth - 1
        lam_init = 0.8 - 0.6 * math.exp(-0.3 * l)
        mod_l = mod[l]
        wn, wt, gn, gt = _layer_params(l, w_in, gq_a, gk_a, gq_b, gk_b, gq_c, gk_c, T_IN)
        g1 = g_norm1[l].reshape(1, D_MODEL)
        g2 = g_norm2[l].reshape(1, D_MODEL)
        lam_vec = jnp.zeros((SUBLANES, LANES), F32).at[0:4, :DIFF_DIM].set(
            jnp.stack([lambda_q1[l], lambda_k1[l], lambda_q2[l], lambda_k2[l]]))
        gsub = jnp.tile(g_subln[l], 2).reshape(1, LANES)
        w_o = w_out[l].astype(BF16)
        ffn_w = (w_up[l].astype(BF16), conv_w[l], conv_b[l].reshape(1, -1), w_down[l].astype(BF16))

        lat = _inproj(x, mod_l, row_x, g1, wn, wt, gn, gt, e64, e32, tabs_x, T_IN)
        cx = _inproj(ctx, mod_l, row_c, g1, wn, wt, gn, gt, e64, e32, tabs_c, T_IN)

        gqa = lambda n_lat, tq: functools.partial(_gqa_kernel, n_lat=n_lat, Tq=tq)
        dif = lambda n_lat, tq: functools.partial(_diff_kernel, n_lat=n_lat, Tq=tq, lam_init=lam_init)
        oa = _flash_call(gqa(SEQ // KC, 256), "gqa_attn", _QTA_BLK, _KA_BLK, _VTA_BLK, QA_W, 256,
                         A_KV_HEADS, A_REP * 256, lat, lat, cx)
        oc = _flash_call(dif(SEQ // KC, 512), "diff_attn", _QTC_BLK, _KC_BLK, _VTC_BLK, QC_W, 512,
                         2, 2 * 512, lat, lat, cx, (lam_vec, gsub))
        ob = _nbr_call(lat, cx, _nbr_bias_tables(rpb_b[l]))
        x = _outproj(x, mod_l, row_x, oa, ob, oc, w_o, 512)
        x = _ffn(x, mod_l, row_x, g2, *ffn_w, 256)

        if with_ctx:
            ta = _flash_call(gqa(0, CTX_LEN), "gqa_attn_ctx", _QTA_BLK, _KA_BLK, _VTA_BLK, QA_W, CTX_LEN,
                             A_KV_HEADS, A_REP * CTX_LEN, cx, None, cx)
            tc = _flash_call(dif(0, CTX_LEN), "diff_attn_ctx", _QTC_BLK, _KC_BLK, _VTC_BLK, QC_W, CTX_LEN,
                             2, 2 * CTX_LEN, cx, None, cx, (lam_vec, gsub))
            tb = _plain_ctx_call(cx)
            ctx = _outproj(ctx, mod_l, row_c, ta, tb, tc, w_o, CTX_LEN)
            ctx = _ffn(ctx, mod_l, row_c, g2, *ffn_w, CTX_LEN)
    return x
```

```python
import functools
import math

import numpy as np
import jax
import jax.numpy as jnp
from jax import lax
from jax.experimental import pallas as pl
from jax.experimental.pallas import tpu as pltpu

F32 = jnp.float32
BF16 = jnp.bfloat16

D_MODEL = 1024
SEQ = 4096
GRID_W = 64
GRID_ROWS = SEQ // GRID_W
CTX_LEN = 256
HEAD_DIM = 64
A_HEADS = 8
A_KV_HEADS = 2
A_REP = A_HEADS // A_KV_HEADS
B_HEADS = 4
C_HEADS = 4
DIFF_DIM = 32
WIN_H = 8
WIN_W = 16
FFN_DIM = 2816
ROPE_THETA = 10000.0
EPS = 1e-6
NEG = -1e30
LOG2E = 1.4426950408889634

LANES = 128
SUBLANES = 8
KC = 256
QTILE = 256
QPASS = 256
FLASH_UNROLL = 8
VMEM_LIMIT = 56 * 1024 * 1024

QA_W, QB_W, QC_W = A_HEADS * HEAD_DIM, B_HEADS * HEAD_DIM, C_HEADS * HEAD_DIM
KA_W = A_KV_HEADS * HEAD_DIM
NAT_COLS = QB_W + QB_W + QC_W + KA_W
_QB_BLK, _VB_BLK, _KC_BLK, _KA_BLK = (QB_W, 0), (QB_W, 1), (QC_W, 2), (KA_W, 6)
TR_ROWS = QA_W + QC_W + QB_W + KA_W + QC_W
QT_ROWS = QA_W + QC_W
_QTA_BLK, _QTC_BLK = (QA_W, 0), (QC_W, 2)
ONES_ROWS = 16
AUG = HEAD_DIM + ONES_ROWS
VT_ROWS = C_HEADS * AUG + A_KV_HEADS * AUG
_VTC_BLK, _VTA_BLK = (C_HEADS * AUG, 0), (A_KV_HEADS * AUG, 2)


def _cparams(sem):
    return pltpu.CompilerParams(dimension_semantics=sem, vmem_limit_bytes=VMEM_LIMIT)


def _dot(a, b):
    return jnp.dot(a, b, preferred_element_type=F32)


def _lane_mask(shape, lo, hi):
    lane = lax.broadcasted_iota(jnp.int32, shape, 1)
    return (lane >= lo) & (lane < hi)


def _mod_kernel(c_ref, w_ref, b_ref, o_ref):
    c = c_ref[...]
    a = c * (1.0 / (1.0 + jnp.exp(-c)))
    a_hi = a.astype(BF16)
    a_lo = (a - a_hi.astype(F32)).astype(BF16)
    w = w_ref[0]
    w_hi = w.astype(BF16)
    w_lo = (w - w_hi.astype(F32)).astype(BF16)
    o_ref[0] = _dot(a_hi, w_hi) + _dot(a_hi, w_lo) + _dot(a_lo, w_hi) + b_ref[0]


def _modulation(c_all, w_ada, b_ada):
    depth = w_ada.shape[0]
    n = w_ada.shape[2]
    nb = 1536
    return pl.pallas_call(
        _mod_kernel,
        out_shape=jax.ShapeDtypeStruct((depth, 16, n), F32),
        grid=(depth, n // nb),
        in_specs=[pl.BlockSpec((16, D_MODEL), lambda l, j: (0, 0)),
                  pl.BlockSpec((1, D_MODEL, nb), lambda l, j: (l, 0, j)),
                  pl.BlockSpec((1, 1, nb), lambda l, j: (l, 0, j))],
        out_specs=pl.BlockSpec((1, 16, nb), lambda l, j: (l, 0, j)),
        compiler_params=_cparams(("arbitrary", "arbitrary")),
        name="adaln_mod",
    )(c_all, w_ada, b_ada.reshape(depth, 1, n))


_NAT_CHUNKS = [(64, None)] * 2 + [(None, None)] * 2 + [(32, "c")] * 2 + [(64, "a")]


def _swap_halves(n, group):
    half = group // 2
    lane = lax.broadcasted_iota(jnp.int32, n.shape, 1)
    first = (lane % group) < half
    return jnp.where(first, pltpu.roll(n, LANES - half, 1), pltpu.roll(n, half, 1))


def _norm_rope_t(a, gain, cos, sin):
    d = a.shape[0]
    ss = jnp.sum(a * a, axis=0, keepdims=True)
    n = a * lax.rsqrt(ss * (1.0 / d) + EPS) * gain
    if cos is None:
        return n
    x1, x2 = n[0:d // 2], n[d // 2:d]
    return jnp.concatenate([x1 * cos - x2 * sin, x1 * sin + x2 * cos], axis=0)


def _inproj_kernel(x_ref, mod_ref, g1_ref, wn_ref, wt_ref, gn_ref, gt_ref, e64_ref, e32_ref,
                   cosa_ref, sina_ref, cosc_ref, sinc_ref, cosat_ref, sinat_ref, cosct_ref, sinct_ref,
                   slab_ref, qt_ref, ktb_ref, vt_ref, *, T):
    x = x_ref[0]
    ms = jnp.mean(x * x, axis=-1, keepdims=True)
    xn = x * lax.rsqrt(ms + EPS) * g1_ref[...]
    h = (xn * (1.0 + mod_ref[0, 1:2, :]) + mod_ref[0, 0:1, :]).astype(BF16)

    nat = _dot(h, wn_ref[...])
    for ci, (group, rope) in enumerate(_NAT_CHUNKS):
        cols = slice(ci * LANES, (ci + 1) * LANES)
        a = nat[:, cols]
        if group is not None:
            y = a * a
            y_hi = y.astype(BF16)
            y_lo = (y - y_hi.astype(F32)).astype(BF16)
            e = e64_ref[...] if group == 64 else e32_ref[...]
            ss = _dot(y_hi, e) + _dot(y_lo, e)
            a = a * lax.rsqrt(ss * (1.0 / group) + EPS) * gn_ref[:, cols]
        if rope == "a":
            a = a * cosa_ref[...] + _swap_halves(a, 64) * sina_ref[...]
        elif rope == "c":
            a = a * cosc_ref[...] + _swap_halves(a, 32) * sinc_ref[...]
        slab_ref[0, :, cols] = a.astype(BF16)

    tr = lax.dot_general(wt_ref[...], h, (((1,), (1,)), ((), ())), preferred_element_type=F32)
    for hd in range(A_HEADS):
        rows = slice(hd * HEAD_DIM, (hd + 1) * HEAD_DIM)
        qt_ref[0, rows, :] = _norm_rope_t(tr[rows], gt_ref[rows, :], cosat_ref[...], sinat_ref[...]).astype(BF16)
    for sh in range(2 * C_HEADS):
        rows = slice(QA_W + sh * DIFF_DIM, QA_W + (sh + 1) * DIFF_DIM)
        qt_ref[0, rows, :] = _norm_rope_t(tr[rows], gt_ref[rows, :], cosct_ref[...], sinct_ref[...]).astype(BF16)
    ones = jnp.ones((ONES_ROWS, KC), BF16)
    for j in range(T // KC):
        tok = slice(j * KC, (j + 1) * KC)
        for hd in range(B_HEADS):
            rows = slice(QT_ROWS + hd * HEAD_DIM, QT_ROWS + (hd + 1) * HEAD_DIM)
            ktb_ref[0, j, hd * HEAD_DIM:(hd + 1) * HEAD_DIM, :] = _norm_rope_t(
                tr[rows, tok], gt_ref[rows, tok], None, None).astype(BF16)
        v0 = QT_ROWS + QB_W
        for i in range(A_KV_HEADS + C_HEADS):
            src = v0 + i * HEAD_DIM
            dst = (C_HEADS + i if i < A_KV_HEADS else i - A_KV_HEADS) * AUG
            vt_ref[0, j, dst:dst + HEAD_DIM, :] = tr[src:src + HEAD_DIM, tok].astype(BF16)
            vt_ref[0, j, dst + HEAD_DIM:dst + AUG, :] = ones


def _inproj(x, mod_l, mod_row, g1, wn, wt, gn, gt, e64, e32, tabs, T):
    b, s, _ = x.shape
    full = lambda shape: pl.BlockSpec(shape, lambda i, t: (0,) * len(shape))
    tab = pl.BlockSpec((T, LANES), lambda i, t: (t, 0))
    tab_t = lambda r: pl.BlockSpec((r, T), lambda i, t: (0, t))
    return pl.pallas_call(
        functools.partial(_inproj_kernel, T=T),
        out_shape=(jax.ShapeDtypeStruct((b, s, NAT_COLS), BF16),
                   jax.ShapeDtypeStruct((b, QT_ROWS, s), BF16),
                   jax.ShapeDtypeStruct((b, s // KC, QB_W, KC), BF16),
                   jax.ShapeDtypeStruct((b, s // KC, VT_ROWS, KC), BF16)),
        grid=(b, s // T),
        in_specs=[pl.BlockSpec((1, T, D_MODEL), lambda i, t: (i, t, 0)),
                  pl.BlockSpec((1, 6, D_MODEL), lambda i, t: (mod_row(i), 0, 0)),
                  full((1, D_MODEL)), full((D_MODEL, NAT_COLS)), full((TR_ROWS, D_MODEL)),
                  full((1, NAT_COLS)), full((QT_ROWS + QB_W, T)), full((LANES, LANES)), full((LANES, LANES)),
                  tab, tab, tab, tab,
                  tab_t(HEAD_DIM // 2), tab_t(HEAD_DIM // 2), tab_t(DIFF_DIM // 2), tab_t(DIFF_DIM // 2)],
        out_specs=(pl.BlockSpec((1, T, NAT_COLS), lambda i, t: (i, t, 0)),
                   pl.BlockSpec((1, QT_ROWS, T), lambda i, t: (i, 0, t)),
                   pl.BlockSpec((1, T // KC, QB_W, KC), lambda i, t: (i, t, 0, 0)),
                   pl.BlockSpec((1, T // KC, VT_ROWS, KC), lambda i, t: (i, t, 0, 0))),
        compiler_params=_cparams(("parallel", "parallel")),
        name="qkv_proj",
    )(x, mod_l, g1, wn, wt, gn, gt, e64, e32, *tabs)


def _tiles(streams):
    return [(i, slice(j, j + QTILE)) for i, (qt, _, _) in enumerate(streams) for j in range(0, qt.shape[1], QTILE)]


def _score_tile(src, k_all, c, k, buf, s_ref, mx_ref):
    i, cols = _tiles(src)[k]
    qt, kcol, _ = src[i]
    row0 = c * KC if isinstance(c, int) else pl.multiple_of(c * KC, KC)
    s = _dot(k_all[pl.ds(row0, KC), kcol:kcol + LANES], qt[:, cols])
    s_ref[buf, i, :, cols] = s
    mx_ref[buf, i, :, cols] = jnp.max(s, axis=0, keepdims=True)


def _flash_t(streams, next_streams, base, k_all, vt_all, n_chunks, s_ref, mx_ref, m_ref, acc_ref):
    for i, (qt, _, _) in enumerate(streams):
        mq = qt.shape[1]
        m_ref[i, :, 0:mq] = jnp.full((1, mq), NEG, F32)
        acc_ref[i, :, 0:mq] = jnp.zeros((AUG, mq), F32)
    tiles = _tiles(streams)

    def update(c, k, buf):
        i, cols = tiles[k]
        vrow = streams[i][2]
        m_prev = m_ref[i, :, cols]
        m_new = jnp.maximum(m_prev, mx_ref[buf, i, :, cols])
        alpha = jnp.exp2(m_prev - m_new)
        p = jnp.exp2((s_ref[buf, i, :, cols] - m_new).astype(BF16))
        acc_ref[i, :, cols] = alpha * acc_ref[i, :, cols] + _dot(vt_all[c, vrow:vrow + AUG, :], p)
        m_ref[i, :, cols] = m_new

    def chunk(c, buf, src, src_c):
        for k in range(len(tiles)):
            if src is not None:
                _score_tile(src, k_all, src_c, k, 1 - buf, s_ref, mx_ref)
            update(c, k, buf)

    def body(i, carry):
        for u in range(FLASH_UNROLL):
            c = FLASH_UNROLL * i + u
            chunk(c, (u + base) % 2, streams, c + 1)
        return carry

    assert n_chunks % FLASH_UNROLL == 0 and FLASH_UNROLL % 2 == 0
    lax.fori_loop(0, n_chunks // FLASH_UNROLL, body, 0)
    chunk(n_chunks, base, next_streams, 0)
    outs = []
    for i, (qt, _, _) in enumerate(streams):
        acc = acc_ref[i, :, 0:qt.shape[1]]
        outs.append(acc[0:HEAD_DIM] / acc[HEAD_DIM:HEAD_DIM + 1])
    return outs


def _run_passes(passes, next_pass, finish, k_all, vt_all, n_lat, scratch):
    s_ref, mx_ref = scratch[0], scratch[1]
    assert next_pass is None or len(passes) % 2 == 0

    @pl.when(pl.program_id(1) == 0)
    def _():
        for k in range(len(_tiles(passes[0]))):
            _score_tile(passes[0], k_all, 0, k, 0, s_ref, mx_ref)

    for p, streams in enumerate(passes):
        nxt = passes[p + 1] if p + 1 < len(passes) else next_pass
        finish(p, _flash_t(streams, nxt, p % 2, k_all, vt_all, n_lat, *scratch))


def _place_rows(x, offset, total=LANES):
    parts = []
    if offset:
        parts.append(jnp.zeros((offset, x.shape[1]), x.dtype))
    parts.append(x)
    if total - offset - x.shape[0]:
        parts.append(jnp.zeros((total - offset - x.shape[0], x.shape[1]), x.dtype))
    return jnp.concatenate(parts, axis=0)


def _gather_kv(refs, n_lat, k_all, vt_all):
    if n_lat:
        kl_ref, kc_ref, vl_ref, vc_ref = refs[:4]
        rest = refs[4:]
    else:
        kc_ref, vc_ref = refs[:2]
        rest = refs[2:]

    @pl.when(pl.program_id(1) == 0)
    def _():
        if n_lat:
            k_all[0:n_lat * KC] = kl_ref[0]
            vt_all[0:n_lat] = vl_ref[0]
        k_all[n_lat * KC:(n_lat + 1) * KC] = kc_ref[0]
        vt_all[n_lat] = vc_ref[0, 0]
    return rest


def _gqa_streams(q_ref, col0):
    streams = []
    for g in range(A_KV_HEADS):
        heads = [_place_rows(q_ref[0, (g * A_REP + r) * HEAD_DIM:(g * A_REP + r + 1) * HEAD_DIM, col0:col0 + QPASS],
                             g * HEAD_DIM) for r in range(A_REP)]
        streams.append((jnp.concatenate(heads, axis=1), 0, g * AUG))
    return streams


def _gqa_kernel(*refs, n_lat, n_pass, has_next):
    k_all, vt_all, *scratch = refs[-6:]
    q_ref = refs[0]
    qn_ref = refs[1] if has_next else None
    (o_ref,) = _gather_kv(refs[1 + has_next:-6], n_lat, k_all, vt_all)

    def finish(p, outs):
        for g in range(A_KV_HEADS):
            for k in range(A_REP // 2):
                pair = jnp.concatenate([outs[g][:, (2 * k) * QPASS:(2 * k + 1) * QPASS],
                                        outs[g][:, (2 * k + 1) * QPASS:(2 * k + 2) * QPASS]], axis=0)
                blk = g * (A_REP // 2) + k
                o_ref[0, p * QPASS:(p + 1) * QPASS, blk * LANES:(blk + 1) * LANES] = pair.T.astype(BF16)

    passes = [_gqa_streams(q_ref, p * QPASS) for p in range(n_pass)]
    _run_passes(passes, _gqa_streams(qn_ref, 0) if has_next else None, finish, k_all, vt_all, n_lat, scratch)


def _diff_streams(q_ref, row0, blk, Tq):
    streams = []
    for half in range(2):
        r = row0 + half * HEAD_DIM
        maps = [_place_rows(q_ref[0, r + mth * DIFF_DIM:r + (mth + 1) * DIFF_DIM, :], half * HEAD_DIM + mth * DIFF_DIM)
                for mth in range(2)]
        streams.append((jnp.concatenate(maps, axis=1), blk * LANES, (2 * blk + half) * AUG))
    return streams


def _diff_kernel(*refs, n_lat, Tq, lam_init, has_next):
    k_all, vt_all, *scratch = refs[-6:]
    q_ref = refs[0]
    qn_ref = refs[1] if has_next else None
    lam_ref, gsub_ref, o_ref = _gather_kv(refs[1 + has_next:-6], n_lat, k_all, vt_all)
    lv = lam_ref[...]
    lam = (jnp.exp(jnp.sum(lv[0:1] * lv[1:2], axis=-1, keepdims=True))
           - jnp.exp(jnp.sum(lv[2:3] * lv[3:4], axis=-1, keepdims=True)) + lam_init)

    def finish(blk, outs):
        normed = []
        for o in outs:
            d = o[:, 0:Tq] - lam * o[:, Tq:2 * Tq]
            ms = jnp.mean(d * d, axis=0, keepdims=True)
            normed.append(d * lax.rsqrt(ms + EPS))
        pair = jnp.concatenate(normed, axis=0).T * (gsub_ref[...] * (1.0 - lam_init))
        o_ref[0, :, blk * LANES:(blk + 1) * LANES] = pair.astype(BF16)

    passes = [_diff_streams(q_ref, blk * LANES, blk, Tq) for blk in range(C_HEADS // 2)]
    _run_passes(passes, _diff_streams(qn_ref, 0, 0, Tq) if has_next else None, finish, k_all, vt_all, n_lat, scratch)


def _flash_call(kernel, name, q_blk, qn_spec, k_blk, vt_blk, out_w, Tq, n_streams, mq, qry, lat, ctx, extra=()):
    qt = qry[1]
    b, _, sq = qt.shape
    ins = [qt]
    specs = [pl.BlockSpec((1, q_blk[0], Tq), lambda i, t: (i, q_blk[1], t))]
    if qn_spec is not None:
        ins.append(qt)
        specs.append(qn_spec)
    k_ctx_spec = pl.BlockSpec((1, CTX_LEN, k_blk[0]), lambda i, t: (i, 0, k_blk[1]))
    v_ctx_spec = pl.BlockSpec((1, 1, vt_blk[0], KC), lambda i, t: (i, 0, vt_blk[1], 0))
    n_lat = 0
    if lat is not None:
        s_lat = lat[0].shape[1]
        n_lat = s_lat // KC
        ins += [lat[0], ctx[0], lat[3], ctx[3]]
        specs += [pl.BlockSpec((1, s_lat, k_blk[0]), lambda i, t: (i, 0, k_blk[1])), k_ctx_spec,
                  pl.BlockSpec((1, s_lat // KC, vt_blk[0], KC), lambda i, t: (i, 0, vt_blk[1], 0)), v_ctx_spec]
    else:
        ins += [ctx[0], ctx[3]]
        specs += [k_ctx_spec, v_ctx_spec]
    for arr in extra:
        ins.append(arr)
        specs.append(pl.BlockSpec(arr.shape, lambda i, t, nd=arr.ndim: (0,) * nd))
    return pl.pallas_call(
        kernel,
        out_shape=jax.ShapeDtypeStruct((b, sq, out_w), BF16),
        grid=(b, sq // Tq),
        in_specs=specs,
        out_specs=pl.BlockSpec((1, Tq, out_w), lambda i, t: (i, t, 0)),
        scratch_shapes=[pltpu.VMEM(((n_lat + 1) * KC, k_blk[0]), BF16),
                        pltpu.VMEM((n_lat + 1, vt_blk[0], KC), BF16),
                        pltpu.VMEM((2, n_streams, KC, mq), F32), pltpu.VMEM((2, n_streams, 1, mq), F32),
                        pltpu.VMEM((n_streams, 1, mq), F32), pltpu.VMEM((n_streams, AUG, mq), F32)],
        compiler_params=_cparams(("parallel", "arbitrary")),
        name=name,
    )(*ins)


NB_ROWS = 8
NB_Q = NB_ROWS * GRID_W
NB_KROWS = 16
NB_K = NB_KROWS * GRID_W
NB_CHUNKS = NB_K // KC


def _nbr_head(q_ref, hd):
    blk, half = hd // 2, hd % 2
    q128 = q_ref[0, :, blk * LANES:(blk + 1) * LANES]
    keep = _lane_mask(q128.shape, half * HEAD_DIM, (half + 1) * HEAD_DIM)
    return jnp.where(keep, q128, jnp.zeros_like(q128))


def _pair(even, odd):
    return jnp.where(_lane_mask(even.shape, 0, HEAD_DIM), even, odd)


def _nbr_kernel(q_ref, ktl_ref, ktc_ref, vl_ref, vc_ref, bias_ref, o_ref, s_ref):
    rb = pl.program_id(1)
    c0 = jnp.clip(2 * rb - 1, 0, SEQ // KC - NB_CHUNKS)
    tiles = []
    for hd in range(B_HEADS):
        blk, half = hd // 2, hd % 2
        rows = slice(blk * LANES, (blk + 1) * LANES)
        q = _nbr_head(q_ref, hd)
        for j in range(NB_CHUNKS):
            s_ref[:, j * KC:(j + 1) * KC] = _dot(q, ktl_ref[0, c0 + j, rows, :]) + bias_ref[0, hd, :, j * KC:(j + 1) * KC]
        s_ref[:, NB_K:] = _dot(q, ktc_ref[0, 0, rows, :])
        s = s_ref[...]
        p = jnp.exp2(s - jnp.max(s, axis=-1, keepdims=True)).astype(BF16)
        vmask = _lane_mask((KC, LANES), half * HEAD_DIM, (half + 1) * HEAD_DIM)
        vc = vc_ref[0, :, rows]
        acc = _dot(p[:, NB_K:], jnp.where(vmask, vc, jnp.ones_like(vc)))
        for j in range(NB_CHUNKS):
            row0 = pl.multiple_of((c0 + j) * KC, KC)
            v = vl_ref[0, pl.ds(row0, KC), rows]
            acc += _dot(p[:, j * KC:(j + 1) * KC], jnp.where(vmask, v, jnp.ones_like(v)))
        tiles.append(acc / pltpu.roll(acc, HEAD_DIM, 1))
        if half == 1:
            o_ref[0, :, rows] = _pair(tiles[-2], tiles[-1]).astype(BF16)


def _nbr_call(lat, ctx, bias):
    slab_l, _, kt_l, _ = lat
    slab_c, _, kt_c, _ = ctx
    b = slab_l.shape[0]
    nl = kt_l.shape[1]
    cls = lambda t: jnp.where(t == 0, 0, jnp.where(t == GRID_ROWS // NB_ROWS - 1, 2, 1))
    return pl.pallas_call(
        _nbr_kernel,
        out_shape=jax.ShapeDtypeStruct((b, SEQ, QB_W), BF16),
        grid=(b, GRID_ROWS // NB_ROWS),
        in_specs=[pl.BlockSpec((1, NB_Q, QB_W), lambda i, t: (i, t, _QB_BLK[1])),
                  pl.BlockSpec((1, nl, QB_W, KC), lambda i, t: (i, 0, 0, 0)),
                  pl.BlockSpec((1, 1, QB_W, KC), lambda i, t: (i, 0, 0, 0)),
                  pl.BlockSpec((1, SEQ, QB_W), lambda i, t: (i, 0, _VB_BLK[1])),
                  pl.BlockSpec((1, CTX_LEN, QB_W), lambda i, t: (i, 0, _VB_BLK[1])),
                  pl.BlockSpec((1, B_HEADS, NB_Q, NB_K), lambda i, t: (cls(t), 0, 0, 0))],
        out_specs=pl.BlockSpec((1, NB_Q, QB_W), lambda i, t: (i, t, 0)),
        scratch_shapes=[pltpu.VMEM((NB_Q, NB_K + CTX_LEN), F32)],
        compiler_params=_cparams(("parallel", "arbitrary")),
        name="nbr_attn",
    )(slab_l, kt_l, kt_c, slab_l, slab_c, bias)


def _plain_ctx_kernel(q_ref, ktc_ref, vc_ref, o_ref):
    tiles = []
    for hd in range(B_HEADS):
        blk, half = hd // 2, hd % 2
        rows = slice(blk * LANES, (blk + 1) * LANES)
        s = _dot(_nbr_head(q_ref, hd), ktc_ref[0, 0, rows, :])
        p = jnp.exp2(s - jnp.max(s, axis=-1, keepdims=True)).astype(BF16)
        vmask = _lane_mask((KC, LANES), half * HEAD_DIM, (half + 1) * HEAD_DIM)
        vc = vc_ref[0, :, rows]
        acc = _dot(p, jnp.where(vmask, vc, jnp.ones_like(vc)))
        tiles.append(acc / pltpu.roll(acc, HEAD_DIM, 1))
        if half == 1:
            o_ref[0, :, rows] = _pair(tiles[-2], tiles[-1]).astype(BF16)


def _plain_ctx_call(ctx):
    slab_c, _, kt_c, _ = ctx
    b = slab_c.shape[0]
    return pl.pallas_call(
        _plain_ctx_kernel,
        out_shape=jax.ShapeDtypeStruct((b, CTX_LEN, QB_W), BF16),
        grid=(b,),
        in_specs=[pl.BlockSpec((1, CTX_LEN, QB_W), lambda i: (i, 0, _QB_BLK[1])),
                  pl.BlockSpec((1, 1, QB_W, KC), lambda i: (i, 0, 0, 0)),
                  pl.BlockSpec((1, CTX_LEN, QB_W), lambda i: (i, 0, _VB_BLK[1]))],
        out_specs=pl.BlockSpec((1, CTX_LEN, QB_W), lambda i: (i, 0, 0)),
        compiler_params=_cparams(("parallel",)),
        name="plain_attn_ctx",
    )(slab_c, kt_c, slab_c)


def _nbr_bias_tables(rpb):
    n_blocks = GRID_ROWS // NB_ROWS
    sel_r = np.zeros((3, NB_ROWS, NB_KROWS, 2 * WIN_H - 1), np.float32)
    for cls, rbk in enumerate((0, 1, n_blocks - 1)):
        ws = min(max(NB_ROWS * rbk - WIN_H // 2, 0), GRID_ROWS - NB_KROWS)
        for rq in range(NB_ROWS):
            r = NB_ROWS * rbk + rq
            rs = min(max(r - WIN_H // 2, 0), GRID_ROWS - WIN_H)
            for kr in range(NB_KROWS):
                if rs <= ws + kr < rs + WIN_H:
                    sel_r[cls, rq, kr, ws + kr - r + WIN_H - 1] = 1.0
    sel_c = np.zeros((GRID_W, GRID_W, 2 * WIN_W - 1), np.float32)
    for c in range(GRID_W):
        cs = min(max(c - WIN_W // 2, 0), GRID_W - WIN_W)
        for kc in range(cs, cs + WIN_W):
            sel_c[c, kc, kc - c + WIN_W - 1] = 1.0
    valid = np.einsum("xqk,cd->xqckd", sel_r.sum(-1), sel_c.sum(-1)) > 0.5
    t = jnp.einsum("xqki,hij->xhqkj", jnp.asarray(sel_r), rpb * LOG2E, precision=lax.Precision.HIGHEST)
    t = jnp.einsum("xhqkj,cdj->xhqckd", t, jnp.asarray(sel_c), precision=lax.Precision.HIGHEST)
    t = jnp.where(jnp.asarray(valid)[:, None], t, NEG)
    return t.reshape(3, rpb.shape[0], NB_Q, NB_K)


def _outproj_kernel(x_ref, mod_ref, oa_ref, ob_ref, oc_ref, w_ref, o_ref):
    y = (_dot(oa_ref[0], w_ref[0:QA_W]) + _dot(ob_ref[0], w_ref[QA_W:QA_W + QB_W])
         + _dot(oc_ref[0], w_ref[QA_W + QB_W:]))
    o_ref[0] = x_ref[0] + mod_ref[0, 2:3, :] * y


def _outproj(x, mod_l, mod_row, oa, ob, oc, w, T):
    b, s, _ = x.shape
    tile = lambda wd: pl.BlockSpec((1, T, wd), lambda i, t: (i, t, 0))
    return pl.pallas_call(
        _outproj_kernel,
        out_shape=jax.ShapeDtypeStruct(x.shape, F32),
        grid=(b, s // T),
        in_specs=[tile(D_MODEL), pl.BlockSpec((1, 6, D_MODEL), lambda i, t: (mod_row(i), 0, 0)),
                  tile(QA_W), tile(QB_W), tile(QC_W),
                  pl.BlockSpec((D_MODEL, D_MODEL), lambda i, t: (0, 0))],
        out_specs=tile(D_MODEL),
        compiler_params=_cparams(("parallel", "parallel")),
        name="out_proj",
    )(x, mod_l, oa, ob, oc, w)


FFN_CHUNK = 256
HALO = SUBLANES


def _ffn_kernel(x_ref, xp_ref, xn_ref, mod_ref, g2_ref, wup_ref, cw_ref, cb_ref, wdn_ref, o_ref, y_ref, a_ref, *, T):
    t = pl.program_id(1)
    nt = pl.num_programs(1)
    x = x_ref[0]
    xa = jnp.concatenate([xp_ref[0], x, xn_ref[0]], axis=0)
    ms = jnp.mean(xa * xa, axis=-1, keepdims=True)
    h = xa * lax.rsqrt(ms + EPS) * g2_ref[...]
    h = h * (1.0 + mod_ref[0, 4:5, :]) + mod_ref[0, 3:4, :]
    row = lax.broadcasted_iota(jnp.int32, (T + 2 * HALO, 1), 0)
    inside = ((row >= HALO) | (t > 0)) & ((row < T + HALO) | (t < nt - 1))
    h = jnp.where(inside, h, 0.0).astype(BF16)

    def conv(slot, col0):
        w = cw_ref[:, col0:col0 + FFN_CHUNK]
        return (y_ref[slot, HALO - 1:HALO - 1 + T] * w[0:1] + y_ref[slot, HALO:HALO + T] * w[1:2]
                + y_ref[slot, HALO + 1:HALO + 1 + T] * w[2:3] + cb_ref[:, col0:col0 + FFN_CHUNK])

    for j in range(FFN_DIM // FFN_CHUNK):
        cu, cg = j * FFN_CHUNK, FFN_DIM + j * FFN_CHUNK
        su, sg = 2 * (j % 2), 2 * (j % 2) + 1
        y_ref[su] = _dot(h, wup_ref[:, cu:cu + FFN_CHUNK])
        y_ref[sg] = _dot(h, wup_ref[:, cg:cg + FFN_CHUNK])
        g = conv(sg, cg)
        a_ref[:, cu:cu + FFN_CHUNK] = (g * (1.0 / (1.0 + jnp.exp(-g))) * conv(su, cu)).astype(BF16)
    o_ref[0] = x + mod_ref[0, 5:6, :] * _dot(a_ref[...], wdn_ref[...])


def _ffn(x, mod_l, mod_row, g2, w_up, conv_w, conv_b, w_down, T):
    b, s, _ = x.shape
    nh = T // HALO
    last = s // HALO - 1
    const = lambda shape: pl.BlockSpec(shape, lambda i, t: (0,) * len(shape), pipeline_mode=pl.Buffered(1))
    return pl.pallas_call(
        functools.partial(_ffn_kernel, T=T),
        out_shape=jax.ShapeDtypeStruct(x.shape, F32),
        grid=(b, s // T),
        in_specs=[pl.BlockSpec((1, T, D_MODEL), lambda i, t: (i, t, 0)),
                  pl.BlockSpec((1, HALO, D_MODEL), lambda i, t: (i, jnp.maximum(t * nh - 1, 0), 0)),
                  pl.BlockSpec((1, HALO, D_MODEL), lambda i, t: (i, jnp.minimum((t + 1) * nh, last), 0)),
                  pl.BlockSpec((1, 6, D_MODEL), lambda i, t: (mod_row(i), 0, 0)),
                  const((1, D_MODEL)), const((D_MODEL, 2 * FFN_DIM)), const((3, 2 * FFN_DIM)),
                  const((1, 2 * FFN_DIM)), const((FFN_DIM, D_MODEL))],
        out_specs=pl.BlockSpec((1, T, D_MODEL), lambda i, t: (i, t, 0)),
        scratch_shapes=[pltpu.VMEM((4, T + 2 * HALO, FFN_CHUNK), F32), pltpu.VMEM((T, FFN_DIM), BF16)],
        compiler_params=_cparams(("parallel", "arbitrary")),
        name="conv_ffn",
    )(x, x, x, mod_l, g2, w_up, conv_w, conv_b, w_down)


def _split_pairs(w, n_groups, dim):
    lead = w.shape[:-1]
    return w.reshape(lead + (n_groups, dim // 2, 2)).swapaxes(-1, -2).reshape(lead + (n_groups * dim,))


def _rope_angles(dim):
    half = dim // 2
    inv = ROPE_THETA ** (-jnp.arange(0, half, 2, dtype=F32) / half)
    t = jnp.arange(SEQ, dtype=jnp.int32)
    row = (t // GRID_W).astype(F32)
    col = (t % GRID_W).astype(F32)
    return jnp.concatenate([row[:, None] * inv, col[:, None] * inv], axis=-1)


def _rope_tables(dim):
    half = dim // 2
    ang = _rope_angles(dim)
    cos = jnp.tile(jnp.cos(ang), (1, LANES // half))
    sin = jnp.tile(jnp.concatenate([-jnp.sin(ang), jnp.sin(ang)], axis=-1), (1, LANES // dim))
    return cos, sin, jnp.cos(ang).T, jnp.sin(ang).T


def _group_ones(group):
    idx = np.arange(LANES) // group
    return jnp.asarray((idx[:, None] == idx[None, :]).astype(np.float32), dtype=BF16)


def _layer_params(l, w_in, gq_a, gk_a, gq_b, gk_b, gq_c, gk_c, T):
    w = w_in[l]
    o = np.cumsum((0, QA_W, KA_W, KA_W, QB_W, QB_W, QB_W, QC_W, QC_W, QC_W))
    qa, ka, va, qb, kb, vb, qc, kc, vc = (w[:, o[i]:o[i + 1]] for i in range(9))
    wn = jnp.concatenate([qb, vb, _split_pairs(kc, 2 * C_HEADS, DIFF_DIM), _split_pairs(ka, A_KV_HEADS, HEAD_DIM)],
                         axis=1).astype(BF16)
    wt = jnp.concatenate([_split_pairs(qa, A_HEADS, HEAD_DIM), _split_pairs(qc, 2 * C_HEADS, DIFF_DIM), kb, va, vc],
                         axis=1).T.astype(BF16)
    gqa = _split_pairs(gq_a[l], 1, HEAD_DIM) * (HEAD_DIM ** -0.5 * LOG2E)
    gka = _split_pairs(gk_a[l], 1, HEAD_DIM)
    gqc = _split_pairs(gq_c[l], 1, DIFF_DIM) * (DIFF_DIM ** -0.5 * LOG2E)
    gkc = _split_pairs(gk_c[l], 1, DIFF_DIM)
    gn = jnp.concatenate([jnp.tile(gq_b[l] * (HEAD_DIM ** -0.5 * LOG2E), B_HEADS), jnp.ones((QB_W,), F32),
                          jnp.tile(gkc, 2 * C_HEADS), jnp.tile(gka, A_KV_HEADS)]).reshape(1, NAT_COLS)
    gt = jnp.concatenate([jnp.tile(gqa, A_HEADS), jnp.tile(gqc, 2 * C_HEADS), jnp.tile(gk_b[l], B_HEADS)])
    return wn, wt, gn, jnp.broadcast_to(gt[:, None], (QT_ROWS + QB_W, T))


def kernel(x, c, ctx, c_ctx, w_ada, b_ada, g_norm1, w_in, gq_a, gk_a, gq_b, gk_b, rpb_b, gq_c, gk_c,
           lambda_q1, lambda_k1, lambda_q2, lambda_k2, g_subln, w_out, g_norm2, w_up, conv_w, conv_b, w_down):
    depth = w_ada.shape[0]
    batch = x.shape[0]
    assert x.shape == (batch, SEQ, D_MODEL) and ctx.shape == (batch, CTX_LEN, D_MODEL) and batch <= 8

    c_all = jnp.zeros((16, D_MODEL), F32).at[:batch].set(c).at[8].set(c_ctx)
    mod = _modulation(c_all, w_ada, b_ada).reshape(depth, 16, 6, D_MODEL)
    row_x = lambda i: i
    row_c = lambda i: 8

    cos_a, sin_a, cos_at, sin_at = _rope_tables(HEAD_DIM)
    cos_c, sin_c, cos_ct, sin_ct = _rope_tables(DIFF_DIM)
    tabs_x = (cos_a, sin_a, cos_c, sin_c, cos_at, sin_at, cos_ct, sin_ct)
    ident = lambda t: jnp.ones((t.shape[0] if t.shape[0] != SEQ else CTX_LEN,
                                t.shape[1] if t.shape[1] != SEQ else CTX_LEN), F32)
    tabs_c = tuple(ident(t) if i % 2 == 0 else jnp.zeros_like(ident(t)) for i, t in enumerate(tabs_x))
    e64, e32 = _group_ones(HEAD_DIM), _group_ones(DIFF_DIM)
    T_IN = KC

    for l in range(depth):
        with_ctx = l < depth - 1
        lam_init = 0.8 - 0.6 * math.exp(-0.3 * l)
        mod_l = mod[l]
        wn, wt, gn, gt = _layer_params(l, w_in, gq_a, gk_a, gq_b, gk_b, gq_c, gk_c, T_IN)
        g1 = g_norm1[l].reshape(1, D_MODEL)
        g2 = g_norm2[l].reshape(1, D_MODEL)
        lam_vec = jnp.zeros((SUBLANES, LANES), F32).at[0:4, :DIFF_DIM].set(
            jnp.stack([lambda_q1[l], lambda_k1[l], lambda_q2[l], lambda_k2[l]]))
        gsub = jnp.tile(g_subln[l], 2).reshape(1, LANES)
        w_o = w_out[l].astype(BF16)
        ffn_w = (w_up[l].astype(BF16), conv_w[l], conv_b[l].reshape(1, -1), w_down[l].astype(BF16))

        lat = _inproj(x, mod_l, row_x, g1, wn, wt, gn, gt, e64, e32, tabs_x, T_IN)
        cx = _inproj(ctx, mod_l, row_c, g1, wn, wt, gn, gt, e64, e32, tabs_c, T_IN)

        nt_a, nt_c = SEQ // (2 * QPASS), SEQ // 512
        qn_a = pl.BlockSpec((1, QA_W, QPASS), lambda i, t: (i, 0, jnp.minimum(2 * (t + 1), 2 * nt_a - 2)))
        qn_c = pl.BlockSpec((1, LANES, 512), lambda i, t: (i, QA_W // LANES, jnp.minimum(t + 1, nt_c - 1)))
        gqa = lambda n_lat, n_pass, nxt: functools.partial(_gqa_kernel, n_lat=n_lat, n_pass=n_pass, has_next=nxt)
        dif = lambda n_lat, tq, nxt: functools.partial(_diff_kernel, n_lat=n_lat, Tq=tq, lam_init=lam_init,
                                                       has_next=nxt)
        oa = _flash_call(gqa(SEQ // KC, 2, True), "gqa_attn", _QTA_BLK, qn_a, _KA_BLK, _VTA_BLK, QA_W, 2 * QPASS,
                         A_KV_HEADS, A_REP * QPASS, lat, lat, cx)
        oc = _flash_call(dif(SEQ // KC, 512, True), "diff_attn", _QTC_BLK, qn_c, _KC_BLK, _VTC_BLK, QC_W, 512,
                         2, 2 * 512, lat, lat, cx, (lam_vec, gsub))
        ob = _nbr_call(lat, cx, _nbr_bias_tables(rpb_b[l]))
        x = _outproj(x, mod_l, row_x, oa, ob, oc, w_o, 512)
        x = _ffn(x, mod_l, row_x, g2, *ffn_w, 256)

        if with_ctx:
            ta = _flash_call(gqa(0, 1, False), "gqa_attn_ctx", _QTA_BLK, None, _KA_BLK, _VTA_BLK, QA_W, CTX_LEN,
                             A_KV_HEADS, A_REP * CTX_LEN, cx, None, cx)
            tc = _flash_call(dif(0, CTX_LEN, False), "diff_attn_ctx", _QTC_BLK, None, _KC_BLK, _VTC_BLK, QC_W,
                             CTX_LEN, 2, 2 * CTX_LEN, cx, None, cx, (lam_vec, gsub))
            tb = _plain_ctx_call(cx)
            ctx = _outproj(ctx, mod_l, row_c, ta, tb, tc, w_o, CTX_LEN)
            ctx = _ffn(ctx, mod_l, row_c, g2, *ffn_w, CTX_LEN)
    return x
```

```python
import functools
import math

import numpy as np
import jax
import jax.numpy as jnp
from jax import lax
from jax.experimental import pallas as pl
from jax.experimental.pallas import tpu as pltpu

F32 = jnp.float32
BF16 = jnp.bfloat16

D_MODEL = 1024
SEQ = 4096
GRID_W = 64
GRID_ROWS = SEQ // GRID_W
CTX_LEN = 256
HEAD_DIM = 64
A_HEADS = 8
A_KV_HEADS = 2
A_REP = A_HEADS // A_KV_HEADS
B_HEADS = 4
C_HEADS = 4
DIFF_DIM = 32
WIN_H = 8
WIN_W = 16
FFN_DIM = 2816
ROPE_THETA = 10000.0
EPS = 1e-6
NEG = -1e30
LOG2E = 1.4426950408889634

LANES = 128
SUBLANES = 8
KC = 256
QTILE = 256
QPASS = 256
FLASH_UNROLL = 8
VMEM_LIMIT = 56 * 1024 * 1024

QA_W, QB_W, QC_W = A_HEADS * HEAD_DIM, B_HEADS * HEAD_DIM, C_HEADS * HEAD_DIM
KA_W = A_KV_HEADS * HEAD_DIM
NAT_COLS = QB_W + QB_W + QC_W + KA_W
_QB_BLK, _VB_BLK, _KC_BLK, _KA_BLK = (QB_W, 0), (QB_W, 1), (QC_W, 2), (KA_W, 6)
_R_QA, _R_QC, _R_KB, _R_QB, _R_KC, _R_KA = 0, 512, 768, 1024, 1280, 1536
GAIN_ROWS = _R_KA + KA_W
_R_VA, _R_VC, _R_VB = GAIN_ROWS, GAIN_ROWS + KA_W, GAIN_ROWS + KA_W + QC_W
IN_ROWS = _R_VB + QB_W
QT_ROWS = QA_W + QC_W
_QTA_BLK, _QTC_BLK = (QA_W, 0), (QC_W, 2)
ONES_ROWS = 16
AUG = HEAD_DIM + ONES_ROWS
VT_ROWS = C_HEADS * AUG + A_KV_HEADS * AUG
_VTC_BLK, _VTA_BLK = (C_HEADS * AUG, 0), (A_KV_HEADS * AUG, 2)


def _cparams(sem):
    return pltpu.CompilerParams(dimension_semantics=sem, vmem_limit_bytes=VMEM_LIMIT)


def _dot(a, b):
    return jnp.dot(a, b, preferred_element_type=F32)


def _lane_mask(shape, lo, hi):
    lane = lax.broadcasted_iota(jnp.int32, shape, 1)
    return (lane >= lo) & (lane < hi)


def _mod_kernel(c_ref, w_ref, b_ref, o_ref):
    c = c_ref[...]
    a = c * (1.0 / (1.0 + jnp.exp(-c)))
    a_hi = a.astype(BF16)
    a_lo = (a - a_hi.astype(F32)).astype(BF16)
    w = w_ref[0]
    w_hi = w.astype(BF16)
    w_lo = (w - w_hi.astype(F32)).astype(BF16)
    o_ref[0] = _dot(a_hi, w_hi) + _dot(a_hi, w_lo) + _dot(a_lo, w_hi) + b_ref[0]


def _modulation(c_all, w_ada, b_ada):
    depth = w_ada.shape[0]
    n = w_ada.shape[2]
    nb = 1536
    return pl.pallas_call(
        _mod_kernel,
        out_shape=jax.ShapeDtypeStruct((depth, 16, n), F32),
        grid=(depth, n // nb),
        in_specs=[pl.BlockSpec((16, D_MODEL), lambda l, j: (0, 0)),
                  pl.BlockSpec((1, D_MODEL, nb), lambda l, j: (l, 0, j)),
                  pl.BlockSpec((1, 1, nb), lambda l, j: (l, 0, j))],
        out_specs=pl.BlockSpec((1, 16, nb), lambda l, j: (l, 0, j)),
        compiler_params=_cparams(("arbitrary", "arbitrary")),
        name="adaln_mod",
    )(c_all, w_ada, b_ada.reshape(depth, 1, n))


def _norm_rope_t(a, gain, cos, sin):
    d = a.shape[0]
    ss = jnp.sum(a * a, axis=0, keepdims=True)
    n = a * lax.rsqrt(ss * (1.0 / d) + EPS) * gain
    if cos is None:
        return n
    x1, x2 = n[0:d // 2], n[d // 2:d]
    return jnp.concatenate([x1 * cos - x2 * sin, x1 * sin + x2 * cos], axis=0)


def _inproj_kernel(x_ref, mod_ref, g1_ref, wt_ref, gt_ref, cosat_ref, sinat_ref, cosct_ref, sinct_ref,
                   slab_ref, qt_ref, ktb_ref, vt_ref, *, T):
    x = x_ref[0]
    ms = jnp.mean(x * x, axis=-1, keepdims=True)
    xn = x * lax.rsqrt(ms + EPS) * g1_ref[...]
    h = (xn * (1.0 + mod_ref[0, 1:2, :]) + mod_ref[0, 0:1, :]).astype(BF16)
    tr = lax.dot_general(wt_ref[...], h, (((1,), (1,)), ((), ())), preferred_element_type=F32)
    rope_a = (cosat_ref[...], sinat_ref[...])
    rope_c = (cosct_ref[...], sinct_ref[...])

    def head(row0, d, rope):
        rows = slice(row0, row0 + d)
        return _norm_rope_t(tr[rows], gt_ref[rows, :], *rope)

    for hd in range(A_HEADS):
        qt_ref[0, hd * HEAD_DIM:(hd + 1) * HEAD_DIM, :] = head(_R_QA + hd * HEAD_DIM, HEAD_DIM, rope_a).astype(BF16)
    for sh in range(2 * C_HEADS):
        qt_ref[0, QA_W + sh * DIFF_DIM:QA_W + (sh + 1) * DIFF_DIM, :] = head(
            _R_QC + sh * DIFF_DIM, DIFF_DIM, rope_c).astype(BF16)
    kb = [head(_R_KB + hd * HEAD_DIM, HEAD_DIM, (None, None)).astype(BF16) for hd in range(B_HEADS)]
    ones = jnp.ones((ONES_ROWS, KC), BF16)
    for j in range(T // KC):
        tok = slice(j * KC, (j + 1) * KC)
        for hd in range(B_HEADS):
            ktb_ref[0, j, hd * HEAD_DIM:(hd + 1) * HEAD_DIM, :] = kb[hd][:, tok]
        for i in range(A_KV_HEADS + C_HEADS):
            src = _R_VA + i * HEAD_DIM
            dst = (C_HEADS + i if i < A_KV_HEADS else i - A_KV_HEADS) * AUG
            vt_ref[0, j, dst:dst + HEAD_DIM, :] = tr[src:src + HEAD_DIM, tok].astype(BF16)
            vt_ref[0, j, dst + HEAD_DIM:dst + AUG, :] = ones

    def store_nat(col0, tile):
        slab_ref[0, :, col0:col0 + LANES] = tile.T.astype(BF16)

    for k in range(QB_W // LANES):
        r0 = _R_QB + k * LANES
        store_nat(k * LANES, jnp.concatenate([head(r0 + i * HEAD_DIM, HEAD_DIM, (None, None)) for i in range(2)], axis=0))
        store_nat(QB_W + k * LANES, tr[_R_VB + k * LANES:_R_VB + (k + 1) * LANES])
    for k in range(QC_W // LANES):
        r0 = _R_KC + k * LANES
        store_nat(2 * QB_W + k * LANES,
                  jnp.concatenate([head(r0 + i * DIFF_DIM, DIFF_DIM, rope_c) for i in range(LANES // DIFF_DIM)], axis=0))
    store_nat(2 * QB_W + QC_W,
              jnp.concatenate([head(_R_KA + i * HEAD_DIM, HEAD_DIM, rope_a) for i in range(A_KV_HEADS)], axis=0))


def _inproj(x, mod_l, mod_row, g1, wt, gt, tabs, T):
    b, s, _ = x.shape
    full = lambda shape: pl.BlockSpec(shape, lambda i, t: (0,) * len(shape))
    tab_t = lambda r: pl.BlockSpec((r, T), lambda i, t: (0, t))
    return pl.pallas_call(
        functools.partial(_inproj_kernel, T=T),
        out_shape=(jax.ShapeDtypeStruct((b, s, NAT_COLS), BF16),
                   jax.ShapeDtypeStruct((b, QT_ROWS, s), BF16),
                   jax.ShapeDtypeStruct((b, s // KC, QB_W, KC), BF16),
                   jax.ShapeDtypeStruct((b, s // KC, VT_ROWS, KC), BF16)),
        grid=(b, s // T),
        in_specs=[pl.BlockSpec((1, T, D_MODEL), lambda i, t: (i, t, 0)),
                  pl.BlockSpec((1, 6, D_MODEL), lambda i, t: (mod_row(i), 0, 0)),
                  full((1, D_MODEL)), full((IN_ROWS, D_MODEL)), full((GAIN_ROWS, T)),
                  tab_t(HEAD_DIM // 2), tab_t(HEAD_DIM // 2), tab_t(DIFF_DIM // 2), tab_t(DIFF_DIM // 2)],
        out_specs=(pl.BlockSpec((1, T, NAT_COLS), lambda i, t: (i, t, 0)),
                   pl.BlockSpec((1, QT_ROWS, T), lambda i, t: (i, 0, t)),
                   pl.BlockSpec((1, T // KC, QB_W, KC), lambda i, t: (i, t, 0, 0)),
                   pl.BlockSpec((1, T // KC, VT_ROWS, KC), lambda i, t: (i, t, 0, 0))),
        compiler_params=_cparams(("parallel", "parallel")),
        name="qkv_proj",
    )(x, mod_l, g1, wt, gt, *tabs)


def _tiles(streams):
    return [(i, slice(j, j + QTILE)) for i, (qt, _, _) in enumerate(streams) for j in range(0, qt.shape[1], QTILE)]


def _score_tile(src, k_all, c, k, buf, s_ref, mx_ref):
    i, cols = _tiles(src)[k]
    qt, kcol, _ = src[i]
    row0 = c * KC if isinstance(c, int) else pl.multiple_of(c * KC, KC)
    s = _dot(k_all[pl.ds(row0, KC), kcol:kcol + LANES], qt[:, cols])
    s_ref[buf, i, :, cols] = s
    mx_ref[buf, i, :, cols] = jnp.max(s, axis=0, keepdims=True)


def _flash_t(streams, next_streams, base, k_all, vt_all, n_chunks, s_ref, mx_ref, m_ref, acc_ref):
    for i, (qt, _, _) in enumerate(streams):
        mq = qt.shape[1]
        m_ref[i, :, 0:mq] = jnp.full((1, mq), NEG, F32)
        acc_ref[i, :, 0:mq] = jnp.zeros((AUG, mq), F32)
    tiles = _tiles(streams)

    def update(c, k, buf):
        i, cols = tiles[k]
        vrow = streams[i][2]
        m_prev = m_ref[i, :, cols]
        m_new = jnp.maximum(m_prev, mx_ref[buf, i, :, cols])
        alpha = jnp.exp2(m_prev - m_new)
        p = jnp.exp2((s_ref[buf, i, :, cols] - m_new).astype(BF16))
        acc_ref[i, :, cols] = alpha * acc_ref[i, :, cols] + _dot(vt_all[c, vrow:vrow + AUG, :], p)
        m_ref[i, :, cols] = m_new

    def chunk(c, buf, src, src_c):
        for k in range(len(tiles)):
            if src is not None:
                _score_tile(src, k_all, src_c, k, 1 - buf, s_ref, mx_ref)
            update(c, k, buf)

    def body(i, carry):
        for u in range(FLASH_UNROLL):
            c = FLASH_UNROLL * i + u
            chunk(c, (u + base) % 2, streams, c + 1)
        return carry

    assert n_chunks % FLASH_UNROLL == 0 and FLASH_UNROLL % 2 == 0
    lax.fori_loop(0, n_chunks // FLASH_UNROLL, body, 0)
    chunk(n_chunks, base, next_streams, 0)
    outs = []
    for i, (qt, _, _) in enumerate(streams):
        acc = acc_ref[i, :, 0:qt.shape[1]]
        outs.append(acc[0:HEAD_DIM] / acc[HEAD_DIM:HEAD_DIM + 1])
    return outs


def _run_passes(passes, next_pass, finish, k_all, vt_all, n_lat, scratch):
    s_ref, mx_ref = scratch[0], scratch[1]
    assert next_pass is None or len(passes) % 2 == 0

    @pl.when(pl.program_id(1) == 0)
    def _():
        for k in range(len(_tiles(passes[0]))):
            _score_tile(passes[0], k_all, 0, k, 0, s_ref, mx_ref)

    for p, streams in enumerate(passes):
        nxt = passes[p + 1] if p + 1 < len(passes) else next_pass
        finish(p, _flash_t(streams, nxt, p % 2, k_all, vt_all, n_lat, *scratch))


def _place_rows(x, offset, total=LANES):
    parts = []
    if offset:
        parts.append(jnp.zeros((offset, x.shape[1]), x.dtype))
    parts.append(x)
    if total - offset - x.shape[0]:
        parts.append(jnp.zeros((total - offset - x.shape[0], x.shape[1]), x.dtype))
    return jnp.concatenate(parts, axis=0)


def _gather_kv(refs, n_lat, k_all, vt_all):
    if n_lat:
        kl_ref, kc_ref, vl_ref, vc_ref = refs[:4]
        rest = refs[4:]
    else:
        kc_ref, vc_ref = refs[:2]
        rest = refs[2:]

    @pl.when(pl.program_id(1) == 0)
    def _():
        if n_lat:
            k_all[0:n_lat * KC] = kl_ref[0]
            vt_all[0:n_lat] = vl_ref[0]
        k_all[n_lat * KC:(n_lat + 1) * KC] = kc_ref[0]
        vt_all[n_lat] = vc_ref[0, 0]
    return rest


def _gqa_streams(q_ref, col0):
    streams = []
    for g in range(A_KV_HEADS):
        heads = [_place_rows(q_ref[0, (g * A_REP + r) * HEAD_DIM:(g * A_REP + r + 1) * HEAD_DIM, col0:col0 + QPASS],
                             g * HEAD_DIM) for r in range(A_REP)]
        streams.append((jnp.concatenate(heads, axis=1), 0, g * AUG))
    return streams


def _gqa_kernel(*refs, n_lat, n_pass, has_next):
    k_all, vt_all, *scratch = refs[-6:]
    q_ref = refs[0]
    qn_ref = refs[1] if has_next else None
    (o_ref,) = _gather_kv(refs[1 + has_next:-6], n_lat, k_all, vt_all)

    def finish(p, outs):
        for g in range(A_KV_HEADS):
            for k in range(A_REP // 2):
                pair = jnp.concatenate([outs[g][:, (2 * k) * QPASS:(2 * k + 1) * QPASS],
                                        outs[g][:, (2 * k + 1) * QPASS:(2 * k + 2) * QPASS]], axis=0)
                blk = g * (A_REP // 2) + k
                o_ref[0, p * QPASS:(p + 1) * QPASS, blk * LANES:(blk + 1) * LANES] = pair.T.astype(BF16)

    passes = [_gqa_streams(q_ref, p * QPASS) for p in range(n_pass)]
    _run_passes(passes, _gqa_streams(qn_ref, 0) if has_next else None, finish, k_all, vt_all, n_lat, scratch)


def _diff_streams(q_ref, row0, blk, Tq):
    streams = []
    for half in range(2):
        r = row0 + half * HEAD_DIM
        maps = [_place_rows(q_ref[0, r + mth * DIFF_DIM:r + (mth + 1) * DIFF_DIM, :], half * HEAD_DIM + mth * DIFF_DIM)
                for mth in range(2)]
        streams.append((jnp.concatenate(maps, axis=1), blk * LANES, (2 * blk + half) * AUG))
    return streams


def _diff_kernel(*refs, n_lat, Tq, lam_init, has_next):
    k_all, vt_all, *scratch = refs[-6:]
    q_ref = refs[0]
    qn_ref = refs[1] if has_next else None
    lam_ref, gsub_ref, o_ref = _gather_kv(refs[1 + has_next:-6], n_lat, k_all, vt_all)
    lv = lam_ref[...]
    lam = (jnp.exp(jnp.sum(lv[0:1] * lv[1:2], axis=-1, keepdims=True))
           - jnp.exp(jnp.sum(lv[2:3] * lv[3:4], axis=-1, keepdims=True)) + lam_init)

    def finish(blk, outs):
        normed = []
        for o in outs:
            d = o[:, 0:Tq] - lam * o[:, Tq:2 * Tq]
            ms = jnp.mean(d * d, axis=0, keepdims=True)
            normed.append(d * lax.rsqrt(ms + EPS))
        pair = jnp.concatenate(normed, axis=0).T * (gsub_ref[...] * (1.0 - lam_init))
        o_ref[0, :, blk * LANES:(blk + 1) * LANES] = pair.astype(BF16)

    passes = [_diff_streams(q_ref, blk * LANES, blk, Tq) for blk in range(C_HEADS // 2)]
    _run_passes(passes, _diff_streams(qn_ref, 0, 0, Tq) if has_next else None, finish, k_all, vt_all, n_lat, scratch)


def _flash_call(kernel, name, q_blk, qn_spec, k_blk, vt_blk, out_w, Tq, n_streams, mq, qry, lat, ctx, extra=()):
    qt = qry[1]
    b, _, sq = qt.shape
    ins = [qt]
    specs = [pl.BlockSpec((1, q_blk[0], Tq), lambda i, t: (i, q_blk[1], t))]
    if qn_spec is not None:
        ins.append(qt)
        specs.append(qn_spec)
    k_ctx_spec = pl.BlockSpec((1, CTX_LEN, k_blk[0]), lambda i, t: (i, 0, k_blk[1]))
    v_ctx_spec = pl.BlockSpec((1, 1, vt_blk[0], KC), lambda i, t: (i, 0, vt_blk[1], 0))
    n_lat = 0
    if lat is not None:
        s_lat = lat[0].shape[1]
        n_lat = s_lat // KC
        ins += [lat[0], ctx[0], lat[3], ctx[3]]
        specs += [pl.BlockSpec((1, s_lat, k_blk[0]), lambda i, t: (i, 0, k_blk[1])), k_ctx_spec,
                  pl.BlockSpec((1, s_lat // KC, vt_blk[0], KC), lambda i, t: (i, 0, vt_blk[1], 0)), v_ctx_spec]
    else:
        ins += [ctx[0], ctx[3]]
        specs += [k_ctx_spec, v_ctx_spec]
    for arr in extra:
        ins.append(arr)
        specs.append(pl.BlockSpec(arr.shape, lambda i, t, nd=arr.ndim: (0,) * nd))
    return pl.pallas_call(
        kernel,
        out_shape=jax.ShapeDtypeStruct((b, sq, out_w), BF16),
        grid=(b, sq // Tq),
        in_specs=specs,
        out_specs=pl.BlockSpec((1, Tq, out_w), lambda i, t: (i, t, 0)),
        scratch_shapes=[pltpu.VMEM(((n_lat + 1) * KC, k_blk[0]), BF16),
                        pltpu.VMEM((n_lat + 1, vt_blk[0], KC), BF16),
                        pltpu.VMEM((2, n_streams, KC, mq), F32), pltpu.VMEM((2, n_streams, 1, mq), F32),
                        pltpu.VMEM((n_streams, 1, mq), F32), pltpu.VMEM((n_streams, AUG, mq), F32)],
        compiler_params=_cparams(("parallel", "arbitrary")),
        name=name,
    )(*ins)


NB_ROWS = 8
NB_Q = NB_ROWS * GRID_W
NB_KROWS = 16
NB_HALF_Q = NB_Q // 2
NB_HALF_KROWS = 12
NB_HALF_K = NB_HALF_KROWS * GRID_W
NB_HALF_CHUNKS = NB_HALF_K // KC


def _nbr_half_offset(block, half):
    last = GRID_ROWS // NB_ROWS - 1
    return (block == last) if half == 0 else (block >= 1)


def _nbr_head(q_ref, hd):
    blk, half = hd // 2, hd % 2
    q128 = q_ref[0, :, blk * LANES:(blk + 1) * LANES]
    keep = _lane_mask(q128.shape, half * HEAD_DIM, (half + 1) * HEAD_DIM)
    return jnp.where(keep, q128, jnp.zeros_like(q128))


def _pair(even, odd):
    return jnp.where(_lane_mask(even.shape, 0, HEAD_DIM), even, odd)


def _nbr_kernel(q_ref, ktl_ref, ktc_ref, vl_ref, vc_ref, bias_ref, o_ref, s_ref):
    rb = pl.program_id(1)
    w0 = jnp.clip(2 * rb - 1, 0, SEQ // KC - NB_KROWS * GRID_W // KC)
    for g in range(2):
        c0 = w0 + _nbr_half_offset(rb, g).astype(jnp.int32)
        qrows = slice(g * NB_HALF_Q, (g + 1) * NB_HALF_Q)
        tiles = []
        for hd in range(B_HEADS):
            blk, half = hd // 2, hd % 2
            rows = slice(blk * LANES, (blk + 1) * LANES)
            q128 = q_ref[0, qrows, rows]
            keep = _lane_mask(q128.shape, half * HEAD_DIM, (half + 1) * HEAD_DIM)
            q = jnp.where(keep, q128, jnp.zeros_like(q128))
            for j in range(NB_HALF_CHUNKS):
                s_ref[:, j * KC:(j + 1) * KC] = (_dot(q, ktl_ref[0, c0 + j, rows, :])
                                                 + bias_ref[0, g, hd, :, j * KC:(j + 1) * KC])
            s_ref[:, NB_HALF_K:] = _dot(q, ktc_ref[0, 0, rows, :])
            s = s_ref[...]
            p = jnp.exp2(s - jnp.max(s, axis=-1, keepdims=True)).astype(BF16)
            vmask = _lane_mask((KC, LANES), half * HEAD_DIM, (half + 1) * HEAD_DIM)
            vc = vc_ref[0, :, rows]
            acc = _dot(p[:, NB_HALF_K:], jnp.where(vmask, vc, jnp.ones_like(vc)))
            for j in range(NB_HALF_CHUNKS):
                row0 = pl.multiple_of((c0 + j) * KC, KC)
                v = vl_ref[0, pl.ds(row0, KC), rows]
                acc += _dot(p[:, j * KC:(j + 1) * KC], jnp.where(vmask, v, jnp.ones_like(v)))
            tiles.append(acc / pltpu.roll(acc, HEAD_DIM, 1))
            if half == 1:
                o_ref[0, qrows, rows] = _pair(tiles[-2], tiles[-1]).astype(BF16)


def _nbr_call(lat, ctx, bias):
    slab_l, _, kt_l, _ = lat
    slab_c, _, kt_c, _ = ctx
    b = slab_l.shape[0]
    nl = kt_l.shape[1]
    cls = lambda t: jnp.where(t == 0, 0, jnp.where(t == GRID_ROWS // NB_ROWS - 1, 2, 1))
    return pl.pallas_call(
        _nbr_kernel,
        out_shape=jax.ShapeDtypeStruct((b, SEQ, QB_W), BF16),
        grid=(b, GRID_ROWS // NB_ROWS),
        in_specs=[pl.BlockSpec((1, NB_Q, QB_W), lambda i, t: (i, t, _QB_BLK[1])),
                  pl.BlockSpec((1, nl, QB_W, KC), lambda i, t: (i, 0, 0, 0)),
                  pl.BlockSpec((1, 1, QB_W, KC), lambda i, t: (i, 0, 0, 0)),
                  pl.BlockSpec((1, SEQ, QB_W), lambda i, t: (i, 0, _VB_BLK[1])),
                  pl.BlockSpec((1, CTX_LEN, QB_W), lambda i, t: (i, 0, _VB_BLK[1])),
                  pl.BlockSpec((1, 2, B_HEADS, NB_HALF_Q, NB_HALF_K), lambda i, t: (cls(t), 0, 0, 0, 0))],
        out_specs=pl.BlockSpec((1, NB_Q, QB_W), lambda i, t: (i, t, 0)),
        scratch_shapes=[pltpu.VMEM((NB_HALF_Q, NB_HALF_K + CTX_LEN), F32)],
        compiler_params=_cparams(("parallel", "arbitrary")),
        name="nbr_attn",
    )(slab_l, kt_l, kt_c, slab_l, slab_c, bias)


def _plain_ctx_kernel(q_ref, ktc_ref, vc_ref, o_ref):
    tiles = []
    for hd in range(B_HEADS):
        blk, half = hd // 2, hd % 2
        rows = slice(blk * LANES, (blk + 1) * LANES)
        s = _dot(_nbr_head(q_ref, hd), ktc_ref[0, 0, rows, :])
        p = jnp.exp2(s - jnp.max(s, axis=-1, keepdims=True)).astype(BF16)
        vmask = _lane_mask((KC, LANES), half * HEAD_DIM, (half + 1) * HEAD_DIM)
        vc = vc_ref[0, :, rows]
        acc = _dot(p, jnp.where(vmask, vc, jnp.ones_like(vc)))
        tiles.append(acc / pltpu.roll(acc, HEAD_DIM, 1))
        if half == 1:
            o_ref[0, :, rows] = _pair(tiles[-2], tiles[-1]).astype(BF16)


def _plain_ctx_call(ctx):
    slab_c, _, kt_c, _ = ctx
    b = slab_c.shape[0]
    return pl.pallas_call(
        _plain_ctx_kernel,
        out_shape=jax.ShapeDtypeStruct((b, CTX_LEN, QB_W), BF16),
        grid=(b,),
        in_specs=[pl.BlockSpec((1, CTX_LEN, QB_W), lambda i: (i, 0, _QB_BLK[1])),
                  pl.BlockSpec((1, 1, QB_W, KC), lambda i: (i, 0, 0, 0)),
                  pl.BlockSpec((1, CTX_LEN, QB_W), lambda i: (i, 0, _VB_BLK[1]))],
        out_specs=pl.BlockSpec((1, CTX_LEN, QB_W), lambda i: (i, 0, 0)),
        compiler_params=_cparams(("parallel",)),
        name="plain_attn_ctx",
    )(slab_c, kt_c, slab_c)


def _nbr_bias_tables(rpb):
    n_blocks = GRID_ROWS // NB_ROWS
    half_rows = NB_ROWS // 2
    sel_r = np.zeros((3, 2, half_rows, NB_HALF_KROWS, 2 * WIN_H - 1), np.float32)
    for cls, rbk in enumerate((0, 1, n_blocks - 1)):
        for g in range(2):
            ws = min(max(NB_ROWS * rbk - WIN_H // 2, 0), GRID_ROWS - NB_KROWS) + 4 * int(_nbr_half_offset(rbk, g))
            for rq in range(half_rows):
                r = NB_ROWS * rbk + g * half_rows + rq
                rs = min(max(r - WIN_H // 2, 0), GRID_ROWS - WIN_H)
                assert ws <= rs and rs + WIN_H <= ws + NB_HALF_KROWS
                for kr in range(rs - ws, rs - ws + WIN_H):
                    sel_r[cls, g, rq, kr, ws + kr - r + WIN_H - 1] = 1.0
    sel_c = np.zeros((GRID_W, GRID_W, 2 * WIN_W - 1), np.float32)
    for c in range(GRID_W):
        cs = min(max(c - WIN_W // 2, 0), GRID_W - WIN_W)
        for kc in range(cs, cs + WIN_W):
            sel_c[c, kc, kc - c + WIN_W - 1] = 1.0
    valid = np.einsum("xgqk,cd->xgqckd", sel_r.sum(-1), sel_c.sum(-1)) > 0.5
    t = jnp.einsum("xgqki,hij->xghqkj", jnp.asarray(sel_r), rpb * LOG2E, precision=lax.Precision.HIGHEST)
    t = jnp.einsum("xghqkj,cdj->xghqckd", t, jnp.asarray(sel_c), precision=lax.Precision.HIGHEST)
    t = jnp.where(jnp.asarray(valid)[:, :, None], t, NEG)
    return t.reshape(3, 2, rpb.shape[0], NB_HALF_Q, NB_HALF_K)


FFN_CHUNK = 256
HALO = SUBLANES
OHALO = 16


def _ffn_kernel(x_ref, xp_ref, xn_ref, oa_ref, oap_ref, oan_ref, ob_ref, obp_ref, obn_ref, oc_ref, ocp_ref, ocn_ref,
                mod_ref, g2_ref, wo_ref, wup_ref, cw_ref, cb_ref, wdn_ref, o_ref, y_ref, a_ref, *, T):
    t = pl.program_id(1)
    nt = pl.num_programs(1)

    def with_halo(prev, main, nxt):
        return jnp.concatenate([prev[0], main[0], nxt[0]], axis=0)

    proj = (_dot(with_halo(oap_ref, oa_ref, oan_ref), wo_ref[0:QA_W])
            + _dot(with_halo(obp_ref, ob_ref, obn_ref), wo_ref[QA_W:QA_W + QB_W])
            + _dot(with_halo(ocp_ref, oc_ref, ocn_ref), wo_ref[QA_W + QB_W:]))
    xa = (with_halo(xp_ref, x_ref, xn_ref)
          + mod_ref[0, 2:3, :] * proj[OHALO - HALO:OHALO + T + HALO])
    x = xa[HALO:HALO + T]
    ms = jnp.mean(xa * xa, axis=-1, keepdims=True)
    h = xa * lax.rsqrt(ms + EPS) * g2_ref[...]
    h = h * (1.0 + mod_ref[0, 4:5, :]) + mod_ref[0, 3:4, :]
    row = lax.broadcasted_iota(jnp.int32, (T + 2 * HALO, 1), 0)
    inside = ((row >= HALO) | (t > 0)) & ((row < T + HALO) | (t < nt - 1))
    h = jnp.where(inside, h, 0.0).astype(BF16)

    def conv(slot, col0):
        w = cw_ref[:, col0:col0 + FFN_CHUNK]
        return (y_ref[slot, HALO - 1:HALO - 1 + T] * w[0:1] + y_ref[slot, HALO:HALO + T] * w[1:2]
                + y_ref[slot, HALO + 1:HALO + 1 + T] * w[2:3] + cb_ref[:, col0:col0 + FFN_CHUNK])

    for j in range(FFN_DIM // FFN_CHUNK):
        cu, cg = j * FFN_CHUNK, FFN_DIM + j * FFN_CHUNK
        su, sg = 2 * (j % 2), 2 * (j % 2) + 1
        y_ref[su] = _dot(h, wup_ref[:, cu:cu + FFN_CHUNK])
        y_ref[sg] = _dot(h, wup_ref[:, cg:cg + FFN_CHUNK])
        g = conv(sg, cg)
        a_ref[:, cu:cu + FFN_CHUNK] = (g * (1.0 / (1.0 + jnp.exp(-g))) * conv(su, cu)).astype(BF16)
    o_ref[0] = x + mod_ref[0, 5:6, :] * _dot(a_ref[...], wdn_ref[...])


def _ffn(x, attn, mod_l, mod_row, g2, w_out, w_up, conv_w, conv_b, w_down, T):
    b, s, _ = x.shape
    const = lambda shape: pl.BlockSpec(shape, lambda i, t: (0,) * len(shape), pipeline_mode=pl.Buffered(1))

    def tile_and_halos(width, halo):
        nh, last = T // halo, s // halo - 1
        return [pl.BlockSpec((1, T, width), lambda i, t: (i, t, 0)),
                pl.BlockSpec((1, halo, width), lambda i, t: (i, jnp.maximum(t * nh - 1, 0), 0)),
                pl.BlockSpec((1, halo, width), lambda i, t: (i, jnp.minimum((t + 1) * nh, last), 0))]

    specs = tile_and_halos(D_MODEL, HALO)
    args = [x, x, x]
    for o in attn:
        specs += tile_and_halos(o.shape[-1], OHALO)
        args += [o, o, o]
    return pl.pallas_call(
        functools.partial(_ffn_kernel, T=T),
        out_shape=jax.ShapeDtypeStruct(x.shape, F32),
        grid=(b, s // T),
        in_specs=specs + [pl.BlockSpec((1, 6, D_MODEL), lambda i, t: (mod_row(i), 0, 0)),
                          const((1, D_MODEL)), const((D_MODEL, D_MODEL)), const((D_MODEL, 2 * FFN_DIM)),
                          const((3, 2 * FFN_DIM)), const((1, 2 * FFN_DIM)), const((FFN_DIM, D_MODEL))],
        out_specs=pl.BlockSpec((1, T, D_MODEL), lambda i, t: (i, t, 0)),
        scratch_shapes=[pltpu.VMEM((4, T + 2 * HALO, FFN_CHUNK), F32), pltpu.VMEM((T, FFN_DIM), BF16)],
        compiler_params=_cparams(("parallel", "arbitrary")),
        name="outproj_conv_ffn",
    )(*args, mod_l, g2, w_out, w_up, conv_w, conv_b, w_down)


def _split_pairs(w, n_groups, dim):
    lead = w.shape[:-1]
    return w.reshape(lead + (n_groups, dim // 2, 2)).swapaxes(-1, -2).reshape(lead + (n_groups * dim,))


def _rope_angles(dim):
    half = dim // 2
    inv = ROPE_THETA ** (-jnp.arange(0, half, 2, dtype=F32) / half)
    t = jnp.arange(SEQ, dtype=jnp.int32)
    row = (t // GRID_W).astype(F32)
    col = (t % GRID_W).astype(F32)
    return jnp.concatenate([row[:, None] * inv, col[:, None] * inv], axis=-1)


def _rope_tables(dim):
    ang = _rope_angles(dim)
    return jnp.cos(ang).T, jnp.sin(ang).T


def _layer_params(l, w_in, gq_a, gk_a, gq_b, gk_b, gq_c, gk_c, T):
    w = w_in[l]
    o = np.cumsum((0, QA_W, KA_W, KA_W, QB_W, QB_W, QB_W, QC_W, QC_W, QC_W))
    qa, ka, va, qb, kb, vb, qc, kc, vc = (w[:, o[i]:o[i + 1]] for i in range(9))
    wt = jnp.concatenate([_split_pairs(qa, A_HEADS, HEAD_DIM), _split_pairs(qc, 2 * C_HEADS, DIFF_DIM), kb, qb,
                          _split_pairs(kc, 2 * C_HEADS, DIFF_DIM), _split_pairs(ka, A_KV_HEADS, HEAD_DIM), va, vc, vb],
                         axis=1).T.astype(BF16)
    gqa = _split_pairs(gq_a[l], 1, HEAD_DIM) * (HEAD_DIM ** -0.5 * LOG2E)
    gka = _split_pairs(gk_a[l], 1, HEAD_DIM)
    gqc = _split_pairs(gq_c[l], 1, DIFF_DIM) * (DIFF_DIM ** -0.5 * LOG2E)
    gkc = _split_pairs(gk_c[l], 1, DIFF_DIM)
    gt = jnp.concatenate([jnp.tile(gqa, A_HEADS), jnp.tile(gqc, 2 * C_HEADS), jnp.tile(gk_b[l], B_HEADS),
                          jnp.tile(gq_b[l] * (HEAD_DIM ** -0.5 * LOG2E), B_HEADS), jnp.tile(gkc, 2 * C_HEADS),
                          jnp.tile(gka, A_KV_HEADS)])
    return wt, jnp.broadcast_to(gt[:, None], (GAIN_ROWS, T))


def kernel(x, c, ctx, c_ctx, w_ada, b_ada, g_norm1, w_in, gq_a, gk_a, gq_b, gk_b, rpb_b, gq_c, gk_c,
           lambda_q1, lambda_k1, lambda_q2, lambda_k2, g_subln, w_out, g_norm2, w_up, conv_w, conv_b, w_down):
    depth = w_ada.shape[0]
    batch = x.shape[0]
    assert x.shape == (batch, SEQ, D_MODEL) and ctx.shape == (batch, CTX_LEN, D_MODEL) and batch <= 8

    c_all = jnp.zeros((16, D_MODEL), F32).at[:batch].set(c).at[8].set(c_ctx)
    mod = _modulation(c_all, w_ada, b_ada).reshape(depth, 16, 6, D_MODEL)
    row_x = lambda i: i
    row_c = lambda i: 8

    tabs_x = _rope_tables(HEAD_DIM) + _rope_tables(DIFF_DIM)
    tabs_c = tuple(jnp.ones((t.shape[0], CTX_LEN), F32) if i % 2 == 0 else jnp.zeros((t.shape[0], CTX_LEN), F32)
                   for i, t in enumerate(tabs_x))
    T_IN = 512

    for l in range(depth):
        with_ctx = l < depth - 1
        lam_init = 0.8 - 0.6 * math.exp(-0.3 * l)
        mod_l = mod[l]
        wt, gt = _layer_params(l, w_in, gq_a, gk_a, gq_b, gk_b, gq_c, gk_c, T_IN)
        g1 = g_norm1[l].reshape(1, D_MODEL)
        g2 = g_norm2[l].reshape(1, D_MODEL)
        lam_vec = jnp.zeros((SUBLANES, LANES), F32).at[0:4, :DIFF_DIM].set(
            jnp.stack([lambda_q1[l], lambda_k1[l], lambda_q2[l], lambda_k2[l]]))
        gsub = jnp.tile(g_subln[l], 2).reshape(1, LANES)
        w_o = w_out[l].astype(BF16)
        ffn_w = (w_up[l].astype(BF16), conv_w[l], conv_b[l].reshape(1, -1), w_down[l].astype(BF16))

        lat = _inproj(x, mod_l, row_x, g1, wt, gt, tabs_x, T_IN)
        cx = _inproj(ctx, mod_l, row_c, g1, wt, gt[:, :CTX_LEN], tabs_c, CTX_LEN)

        nt_a, nt_c = SEQ // (2 * QPASS), SEQ // 512
        qn_a = pl.BlockSpec((1, QA_W, QPASS), lambda i, t: (i, 0, jnp.minimum(2 * (t + 1), 2 * nt_a - 2)))
        qn_c = pl.BlockSpec((1, LANES, 512), lambda i, t: (i, QA_W // LANES, jnp.minimum(t + 1, nt_c - 1)))
        gqa = lambda n_lat, n_pass, nxt: functools.partial(_gqa_kernel, n_lat=n_lat, n_pass=n_pass, has_next=nxt)
        dif = lambda n_lat, tq, nxt: functools.partial(_diff_kernel, n_lat=n_lat, Tq=tq, lam_init=lam_init,
                                                       has_next=nxt)
        oa = _flash_call(gqa(SEQ // KC, 2, True), "gqa_attn", _QTA_BLK, qn_a, _KA_BLK, _VTA_BLK, QA_W, 2 * QPASS,
                         A_KV_HEADS, A_REP * QPASS, lat, lat, cx)
        oc = _flash_call(dif(SEQ // KC, 512, True), "diff_attn", _QTC_BLK, qn_c, _KC_BLK, _VTC_BLK, QC_W, 512,
                         2, 2 * 512, lat, lat, cx, (lam_vec, gsub))
        ob = _nbr_call(lat, cx, _nbr_bias_tables(rpb_b[l]))
        x = _ffn(x, (oa, ob, oc), mod_l, row_x, g2, w_o, *ffn_w, 512)

        if with_ctx:
            ta = _flash_call(gqa(0, 1, False), "gqa_attn_ctx", _QTA_BLK, None, _KA_BLK, _VTA_BLK, QA_W, CTX_LEN,
                             A_KV_HEADS, A_REP * CTX_LEN, cx, None, cx)
            tc = _flash_call(dif(0, CTX_LEN, False), "diff_attn_ctx", _QTC_BLK, None, _KC_BLK, _VTC_BLK, QC_W,
                             CTX_LEN, 2, 2 * CTX_LEN, cx, None, cx, (lam_vec, gsub))
            tb = _plain_ctx_call(cx)
            ctx = _ffn(ctx, (ta, tb, tc), mod_l, row_c, g2, w_o, *ffn_w, CTX_LEN)
    return x
```

```python
import functools
import math

import numpy as np
import jax
import jax.numpy as jnp
from jax import lax
from jax.experimental import pallas as pl
from jax.experimental.pallas import tpu as pltpu

F32 = jnp.float32
BF16 = jnp.bfloat16

D_MODEL = 1024
SEQ = 4096
GRID_W = 64
GRID_ROWS = SEQ // GRID_W
CTX_LEN = 256
HEAD_DIM = 64
A_HEADS = 8
A_KV_HEADS = 2
A_REP = A_HEADS // A_KV_HEADS
B_HEADS = 4
C_HEADS = 4
DIFF_DIM = 32
WIN_H = 8
WIN_W = 16
FFN_DIM = 2816
ROPE_THETA = 10000.0
EPS = 1e-6
NEG = -1e30
LOG2E = 1.4426950408889634

LANES = 128
SUBLANES = 8
KC = 256
QTILE = 256
QPASS = 256
FLASH_UNROLL = 8
VMEM_LIMIT = 56 * 1024 * 1024

QA_W, QB_W, QC_W = A_HEADS * HEAD_DIM, B_HEADS * HEAD_DIM, C_HEADS * HEAD_DIM
KA_W = A_KV_HEADS * HEAD_DIM
NAT_COLS = QB_W + QB_W + QC_W + KA_W
_QB_BLK, _VB_BLK, _KC_BLK, _KA_BLK = (QB_W, 0), (QB_W, 1), (QC_W, 2), (KA_W, 6)
_R_QA, _R_QC, _R_KB, _R_QB, _R_KC, _R_KA = 0, 512, 768, 1024, 1280, 1536
GAIN_ROWS = _R_KA + KA_W
_R_VA, _R_VC, _R_VB = GAIN_ROWS, GAIN_ROWS + KA_W, GAIN_ROWS + KA_W + QC_W
IN_ROWS = _R_VB + QB_W
QT_ROWS = QA_W + QC_W
_QTA_BLK, _QTC_BLK = (QA_W, 0), (QC_W, 2)
ONES_ROWS = 16
AUG = HEAD_DIM + ONES_ROWS
VT_ROWS = C_HEADS * AUG + A_KV_HEADS * AUG
_VTC_BLK, _VTA_BLK = (C_HEADS * AUG, 0), (A_KV_HEADS * AUG, 2)


def _cparams(sem):
    return pltpu.CompilerParams(dimension_semantics=sem, vmem_limit_bytes=VMEM_LIMIT)


def _dot(a, b):
    return jnp.dot(a, b, preferred_element_type=F32)


def _lane_mask(shape, lo, hi):
    lane = lax.broadcasted_iota(jnp.int32, shape, 1)
    return (lane >= lo) & (lane < hi)


def _mod_kernel(c_ref, w_ref, b_ref, o_ref):
    c = c_ref[...]
    a = c * (1.0 / (1.0 + jnp.exp(-c)))
    a_hi = a.astype(BF16)
    a_lo = (a - a_hi.astype(F32)).astype(BF16)
    w = w_ref[0]
    w_hi = w.astype(BF16)
    w_lo = (w - w_hi.astype(F32)).astype(BF16)
    o_ref[0] = _dot(a_hi, w_hi) + _dot(a_hi, w_lo) + _dot(a_lo, w_hi) + b_ref[0]


def _modulation(c_all, w_ada, b_ada):
    depth = w_ada.shape[0]
    n = w_ada.shape[2]
    nb = 1536
    return pl.pallas_call(
        _mod_kernel,
        out_shape=jax.ShapeDtypeStruct((depth, 16, n), F32),
        grid=(depth, n // nb),
        in_specs=[pl.BlockSpec((16, D_MODEL), lambda l, j: (0, 0)),
                  pl.BlockSpec((1, D_MODEL, nb), lambda l, j: (l, 0, j)),
                  pl.BlockSpec((1, 1, nb), lambda l, j: (l, 0, j))],
        out_specs=pl.BlockSpec((1, 16, nb), lambda l, j: (l, 0, j)),
        compiler_params=_cparams(("arbitrary", "arbitrary")),
        name="adaln_mod",
    )(c_all, w_ada, b_ada.reshape(depth, 1, n))


def _norm_rope_t(a, gain, cos, sin):
    d = a.shape[0]
    ss = jnp.sum(a * a, axis=0, keepdims=True)
    n = a * lax.rsqrt(ss * (1.0 / d) + EPS) * gain
    if cos is None:
        return n
    x1, x2 = n[0:d // 2], n[d // 2:d]
    return jnp.concatenate([x1 * cos - x2 * sin, x1 * sin + x2 * cos], axis=0)


def _inproj_kernel(x_ref, mod_ref, g1_ref, wt_ref, gt_ref, cosat_ref, sinat_ref, cosct_ref, sinct_ref,
                   slab_ref, qt_ref, ktb_ref, vt_ref, *, T):
    x = x_ref[0]
    ms = jnp.mean(x * x, axis=-1, keepdims=True)
    xn = x * lax.rsqrt(ms + EPS) * g1_ref[...]
    h = (xn * (1.0 + mod_ref[0, 1:2, :]) + mod_ref[0, 0:1, :]).astype(BF16)
    tr = lax.dot_general(wt_ref[...], h, (((1,), (1,)), ((), ())), preferred_element_type=F32)
    rope_a = (cosat_ref[...], sinat_ref[...])
    rope_c = (cosct_ref[...], sinct_ref[...])

    def head(row0, d, rope):
        rows = slice(row0, row0 + d)
        return _norm_rope_t(tr[rows], gt_ref[rows, :], *rope)

    for hd in range(A_HEADS):
        qt_ref[0, hd * HEAD_DIM:(hd + 1) * HEAD_DIM, :] = head(_R_QA + hd * HEAD_DIM, HEAD_DIM, rope_a).astype(BF16)
    for sh in range(2 * C_HEADS):
        qt_ref[0, QA_W + sh * DIFF_DIM:QA_W + (sh + 1) * DIFF_DIM, :] = head(
            _R_QC + sh * DIFF_DIM, DIFF_DIM, rope_c).astype(BF16)
    kb = [head(_R_KB + hd * HEAD_DIM, HEAD_DIM, (None, None)).astype(BF16) for hd in range(B_HEADS)]
    ones = jnp.ones((ONES_ROWS, KC), BF16)
    for j in range(T // KC):
        tok = slice(j * KC, (j + 1) * KC)
        for hd in range(B_HEADS):
            ktb_ref[0, j, hd * HEAD_DIM:(hd + 1) * HEAD_DIM, :] = kb[hd][:, tok]
        for i in range(A_KV_HEADS + C_HEADS):
            src = _R_VA + i * HEAD_DIM
            dst = (C_HEADS + i if i < A_KV_HEADS else i - A_KV_HEADS) * AUG
            vt_ref[0, j, dst:dst + HEAD_DIM, :] = tr[src:src + HEAD_DIM, tok].astype(BF16)
            vt_ref[0, j, dst + HEAD_DIM:dst + AUG, :] = ones

    def store_nat(col0, tile):
        slab_ref[0, :, col0:col0 + LANES] = tile.T.astype(BF16)

    for k in range(QB_W // LANES):
        r0 = _R_QB + k * LANES
        store_nat(k * LANES, jnp.concatenate([head(r0 + i * HEAD_DIM, HEAD_DIM, (None, None)) for i in range(2)], axis=0))
        store_nat(QB_W + k * LANES, tr[_R_VB + k * LANES:_R_VB + (k + 1) * LANES])
    for k in range(QC_W // LANES):
        r0 = _R_KC + k * LANES
        store_nat(2 * QB_W + k * LANES,
                  jnp.concatenate([head(r0 + i * DIFF_DIM, DIFF_DIM, rope_c) for i in range(LANES // DIFF_DIM)], axis=0))
    store_nat(2 * QB_W + QC_W,
              jnp.concatenate([head(_R_KA + i * HEAD_DIM, HEAD_DIM, rope_a) for i in range(A_KV_HEADS)], axis=0))


def _inproj(x, mod_l, mod_row, g1, wt, gt, tabs, T):
    b, s, _ = x.shape
    full = lambda shape: pl.BlockSpec(shape, lambda i, t: (0,) * len(shape))
    tab_t = lambda r: pl.BlockSpec((r, T), lambda i, t: (0, t))
    return pl.pallas_call(
        functools.partial(_inproj_kernel, T=T),
        out_shape=(jax.ShapeDtypeStruct((b, s, NAT_COLS), BF16),
                   jax.ShapeDtypeStruct((b, QT_ROWS, s), BF16),
                   jax.ShapeDtypeStruct((b, s // KC, QB_W, KC), BF16),
                   jax.ShapeDtypeStruct((b, s // KC, VT_ROWS, KC), BF16)),
        grid=(b, s // T),
        in_specs=[pl.BlockSpec((1, T, D_MODEL), lambda i, t: (i, t, 0)),
                  pl.BlockSpec((1, 6, D_MODEL), lambda i, t: (mod_row(i), 0, 0)),
                  full((1, D_MODEL)), full((IN_ROWS, D_MODEL)), full((GAIN_ROWS, T)),
                  tab_t(HEAD_DIM // 2), tab_t(HEAD_DIM // 2), tab_t(DIFF_DIM // 2), tab_t(DIFF_DIM // 2)],
        out_specs=(pl.BlockSpec((1, T, NAT_COLS), lambda i, t: (i, t, 0)),
                   pl.BlockSpec((1, QT_ROWS, T), lambda i, t: (i, 0, t)),
                   pl.BlockSpec((1, T // KC, QB_W, KC), lambda i, t: (i, t, 0, 0)),
                   pl.BlockSpec((1, T // KC, VT_ROWS, KC), lambda i, t: (i, t, 0, 0))),
        compiler_params=_cparams(("parallel", "parallel")),
        name="qkv_proj",
    )(x, mod_l, g1, wt, gt, *tabs)


def _tiles(streams):
    return [(i, slice(j, j + QTILE)) for i, (qt, _, _) in enumerate(streams) for j in range(0, qt.shape[1], QTILE)]


def _score_tile(src, k_all, c, k, buf, s_ref, mx_ref):
    i, cols = _tiles(src)[k]
    qt, kcol, _ = src[i]
    row0 = c * KC if isinstance(c, int) else pl.multiple_of(c * KC, KC)
    s = _dot(k_all[pl.ds(row0, KC), kcol:kcol + LANES], qt[:, cols])
    s_ref[buf, i, :, cols] = s
    mx_ref[buf, i, :, cols] = jnp.max(s, axis=0, keepdims=True)


def _flash_t(streams, next_streams, base, k_all, vt_all, n_chunks, s_ref, mx_ref, m_ref, acc_ref):
    for i, (qt, _, _) in enumerate(streams):
        mq = qt.shape[1]
        m_ref[i, :, 0:mq] = jnp.full((1, mq), NEG, F32)
        acc_ref[i, :, 0:mq] = jnp.zeros((AUG, mq), F32)
    tiles = _tiles(streams)

    def update(c, k, buf):
        i, cols = tiles[k]
        vrow = streams[i][2]
        m_prev = m_ref[i, :, cols]
        m_new = jnp.maximum(m_prev, mx_ref[buf, i, :, cols])
        alpha = jnp.exp2(m_prev - m_new)
        p = jnp.exp2((s_ref[buf, i, :, cols] - m_new).astype(BF16))
        acc_ref[i, :, cols] = alpha * acc_ref[i, :, cols] + _dot(vt_all[c, vrow:vrow + AUG, :], p)
        m_ref[i, :, cols] = m_new

    def chunk(c, buf, src, src_c):
        for k in range(len(tiles)):
            if src is not None:
                _score_tile(src, k_all, src_c, k, 1 - buf, s_ref, mx_ref)
            update(c, k, buf)

    def body(i, carry):
        for u in range(FLASH_UNROLL):
            c = FLASH_UNROLL * i + u
            chunk(c, (u + base) % 2, streams, c + 1)
        return carry

    assert n_chunks % FLASH_UNROLL == 0 and FLASH_UNROLL % 2 == 0
    lax.fori_loop(0, n_chunks // FLASH_UNROLL, body, 0)
    chunk(n_chunks, base, next_streams, 0)
    outs = []
    for i, (qt, _, _) in enumerate(streams):
        acc = acc_ref[i, :, 0:qt.shape[1]]
        outs.append(acc[0:HEAD_DIM] / acc[HEAD_DIM:HEAD_DIM + 1])
    return outs


def _run_passes(passes, next_pass, finish, k_all, vt_all, n_lat, scratch):
    s_ref, mx_ref = scratch[0], scratch[1]
    assert next_pass is None or len(passes) % 2 == 0

    @pl.when(pl.program_id(1) == 0)
    def _():
        for k in range(len(_tiles(passes[0]))):
            _score_tile(passes[0], k_all, 0, k, 0, s_ref, mx_ref)

    for p, streams in enumerate(passes):
        nxt = passes[p + 1] if p + 1 < len(passes) else next_pass
        finish(p, _flash_t(streams, nxt, p % 2, k_all, vt_all, n_lat, *scratch))


def _place_rows(x, offset, total=LANES):
    parts = []
    if offset:
        parts.append(jnp.zeros((offset, x.shape[1]), x.dtype))
    parts.append(x)
    if total - offset - x.shape[0]:
        parts.append(jnp.zeros((total - offset - x.shape[0], x.shape[1]), x.dtype))
    return jnp.concatenate(parts, axis=0)


def _gather_kv(refs, n_lat, k_all, vt_all):
    if n_lat:
        kl_ref, kc_ref, vl_ref, vc_ref = refs[:4]
        rest = refs[4:]
    else:
        kc_ref, vc_ref = refs[:2]
        rest = refs[2:]

    @pl.when(pl.program_id(1) == 0)
    def _():
        if n_lat:
            k_all[0:n_lat * KC] = kl_ref[0]
            vt_all[0:n_lat] = vl_ref[0]
        k_all[n_lat * KC:(n_lat + 1) * KC] = kc_ref[0]
        vt_all[n_lat] = vc_ref[0, 0]
    return rest


def _gqa_streams(q_ref, col0):
    streams = []
    for g in range(A_KV_HEADS):
        heads = [_place_rows(q_ref[0, (g * A_REP + r) * HEAD_DIM:(g * A_REP + r + 1) * HEAD_DIM, col0:col0 + QPASS],
                             g * HEAD_DIM) for r in range(A_REP)]
        streams.append((jnp.concatenate(heads, axis=1), 0, g * AUG))
    return streams


def _gqa_kernel(*refs, n_lat, n_pass, has_next):
    k_all, vt_all, *scratch = refs[-6:]
    q_ref = refs[0]
    qn_ref = refs[1] if has_next else None
    (o_ref,) = _gather_kv(refs[1 + has_next:-6], n_lat, k_all, vt_all)

    def finish(p, outs):
        for g in range(A_KV_HEADS):
            for k in range(A_REP // 2):
                pair = jnp.concatenate([outs[g][:, (2 * k) * QPASS:(2 * k + 1) * QPASS],
                                        outs[g][:, (2 * k + 1) * QPASS:(2 * k + 2) * QPASS]], axis=0)
                blk = g * (A_REP // 2) + k
                o_ref[0, p * QPASS:(p + 1) * QPASS, blk * LANES:(blk + 1) * LANES] = pair.T.astype(BF16)

    passes = [_gqa_streams(q_ref, p * QPASS) for p in range(n_pass)]
    _run_passes(passes, _gqa_streams(qn_ref, 0) if has_next else None, finish, k_all, vt_all, n_lat, scratch)


def _diff_streams(q_ref, row0, blk, Tq):
    streams = []
    for half in range(2):
        r = row0 + half * HEAD_DIM
        maps = [_place_rows(q_ref[0, r + mth * DIFF_DIM:r + (mth + 1) * DIFF_DIM, :], half * HEAD_DIM + mth * DIFF_DIM)
                for mth in range(2)]
        streams.append((jnp.concatenate(maps, axis=1), blk * LANES, (2 * blk + half) * AUG))
    return streams


def _diff_kernel(*refs, n_lat, Tq, lam_init, has_next):
    k_all, vt_all, *scratch = refs[-6:]
    q_ref = refs[0]
    qn_ref = refs[1] if has_next else None
    lam_ref, gsub_ref, o_ref = _gather_kv(refs[1 + has_next:-6], n_lat, k_all, vt_all)
    lv = lam_ref[...]
    lam = (jnp.exp(jnp.sum(lv[0:1] * lv[1:2], axis=-1, keepdims=True))
           - jnp.exp(jnp.sum(lv[2:3] * lv[3:4], axis=-1, keepdims=True)) + lam_init)

    def finish(blk, outs):
        normed = []
        for o in outs:
            d = o[:, 0:Tq] - lam * o[:, Tq:2 * Tq]
            ms = jnp.mean(d * d, axis=0, keepdims=True)
            normed.append(d * lax.rsqrt(ms + EPS))
        pair = jnp.concatenate(normed, axis=0).T * (gsub_ref[...] * (1.0 - lam_init))
        o_ref[0, :, blk * LANES:(blk + 1) * LANES] = pair.astype(BF16)

    passes = [_diff_streams(q_ref, blk * LANES, blk, Tq) for blk in range(C_HEADS // 2)]
    _run_passes(passes, _diff_streams(qn_ref, 0, 0, Tq) if has_next else None, finish, k_all, vt_all, n_lat, scratch)


def _flash_call(kernel, name, q_blk, qn_spec, k_blk, vt_blk, out_w, Tq, n_streams, mq, qry, lat, ctx, extra=()):
    qt = qry[1]
    b, _, sq = qt.shape
    ins = [qt]
    specs = [pl.BlockSpec((1, q_blk[0], Tq), lambda i, t: (i, q_blk[1], t))]
    if qn_spec is not None:
        ins.append(qt)
        specs.append(qn_spec)
    k_ctx_spec = pl.BlockSpec((1, CTX_LEN, k_blk[0]), lambda i, t: (i, 0, k_blk[1]))
    v_ctx_spec = pl.BlockSpec((1, 1, vt_blk[0], KC), lambda i, t: (i, 0, vt_blk[1], 0))
    n_lat = 0
    if lat is not None:
        s_lat = lat[0].shape[1]
        n_lat = s_lat // KC
        ins += [lat[0], ctx[0], lat[3], ctx[3]]
        specs += [pl.BlockSpec((1, s_lat, k_blk[0]), lambda i, t: (i, 0, k_blk[1])), k_ctx_spec,
                  pl.BlockSpec((1, s_lat // KC, vt_blk[0], KC), lambda i, t: (i, 0, vt_blk[1], 0)), v_ctx_spec]
    else:
        ins += [ctx[0], ctx[3]]
        specs += [k_ctx_spec, v_ctx_spec]
    for arr in extra:
        ins.append(arr)
        specs.append(pl.BlockSpec(arr.shape, lambda i, t, nd=arr.ndim: (0,) * nd))
    return pl.pallas_call(
        kernel,
        out_shape=jax.ShapeDtypeStruct((b, sq, out_w), BF16),
        grid=(b, sq // Tq),
        in_specs=specs,
        out_specs=pl.BlockSpec((1, Tq, out_w), lambda i, t: (i, t, 0)),
        scratch_shapes=[pltpu.VMEM(((n_lat + 1) * KC, k_blk[0]), BF16),
                        pltpu.VMEM((n_lat + 1, vt_blk[0], KC), BF16),
                        pltpu.VMEM((2, n_streams, KC, mq), F32), pltpu.VMEM((2, n_streams, 1, mq), F32),
                        pltpu.VMEM((n_streams, 1, mq), F32), pltpu.VMEM((n_streams, AUG, mq), F32)],
        compiler_params=_cparams(("parallel", "arbitrary")),
        name=name,
    )(*ins)


NB_ROWS = 8
NB_Q = NB_ROWS * GRID_W
NB_KROWS = 16
NB_HALF_Q = NB_Q // 2
NB_HALF_KROWS = 12
NB_HALF_K = NB_HALF_KROWS * GRID_W
NB_HALF_CHUNKS = NB_HALF_K // KC


def _nbr_half_offset(block, half):
    last = GRID_ROWS // NB_ROWS - 1
    return (block == last) if half == 0 else (block >= 1)


def _nbr_head(q_ref, hd):
    blk, half = hd // 2, hd % 2
    q128 = q_ref[0, :, blk * LANES:(blk + 1) * LANES]
    keep = _lane_mask(q128.shape, half * HEAD_DIM, (half + 1) * HEAD_DIM)
    return jnp.where(keep, q128, jnp.zeros_like(q128))


def _pair(even, odd):
    return jnp.where(_lane_mask(even.shape, 0, HEAD_DIM), even, odd)


def _nbr_kernel(q_ref, ktl_ref, ktc_ref, vl_ref, vc_ref, bias_ref, o_ref, s_ref):
    rb = pl.program_id(1)
    w0 = jnp.clip(2 * rb - 1, 0, SEQ // KC - NB_KROWS * GRID_W // KC)
    for g in range(2):
        c0 = w0 + _nbr_half_offset(rb, g).astype(jnp.int32)
        qrows = slice(g * NB_HALF_Q, (g + 1) * NB_HALF_Q)
        tiles = []
        for hd in range(B_HEADS):
            blk, half = hd // 2, hd % 2
            rows = slice(blk * LANES, (blk + 1) * LANES)
            q128 = q_ref[0, qrows, rows]
            keep = _lane_mask(q128.shape, half * HEAD_DIM, (half + 1) * HEAD_DIM)
            q = jnp.where(keep, q128, jnp.zeros_like(q128))
            for j in range(NB_HALF_CHUNKS):
                s_ref[:, j * KC:(j + 1) * KC] = (_dot(q, ktl_ref[0, c0 + j, rows, :])
                                                 + bias_ref[0, g, hd, :, j * KC:(j + 1) * KC])
            s_ref[:, NB_HALF_K:] = _dot(q, ktc_ref[0, 0, rows, :])
            s = s_ref[...]
            p = jnp.exp2(s - jnp.max(s, axis=-1, keepdims=True)).astype(BF16)
            vmask = _lane_mask((KC, LANES), half * HEAD_DIM, (half + 1) * HEAD_DIM)
            vc = vc_ref[0, :, rows]
            acc = _dot(p[:, NB_HALF_K:], jnp.where(vmask, vc, jnp.ones_like(vc)))
            for j in range(NB_HALF_CHUNKS):
                row0 = pl.multiple_of((c0 + j) * KC, KC)
                v = vl_ref[0, pl.ds(row0, KC), rows]
                acc += _dot(p[:, j * KC:(j + 1) * KC], jnp.where(vmask, v, jnp.ones_like(v)))
            tiles.append(acc / pltpu.roll(acc, HEAD_DIM, 1))
            if half == 1:
                o_ref[0, qrows, rows] = _pair(tiles[-2], tiles[-1]).astype(BF16)


def _nbr_call(lat, ctx, bias):
    slab_l, _, kt_l, _ = lat
    slab_c, _, kt_c, _ = ctx
    b = slab_l.shape[0]
    nl = kt_l.shape[1]
    cls = lambda t: jnp.where(t == 0, 0, jnp.where(t == GRID_ROWS // NB_ROWS - 1, 2, 1))
    return pl.pallas_call(
        _nbr_kernel,
        out_shape=jax.ShapeDtypeStruct((b, SEQ, QB_W), BF16),
        grid=(b, GRID_ROWS // NB_ROWS),
        in_specs=[pl.BlockSpec((1, NB_Q, QB_W), lambda i, t: (i, t, _QB_BLK[1])),
                  pl.BlockSpec((1, nl, QB_W, KC), lambda i, t: (i, 0, 0, 0)),
                  pl.BlockSpec((1, 1, QB_W, KC), lambda i, t: (i, 0, 0, 0)),
                  pl.BlockSpec((1, SEQ, QB_W), lambda i, t: (i, 0, _VB_BLK[1])),
                  pl.BlockSpec((1, CTX_LEN, QB_W), lambda i, t: (i, 0, _VB_BLK[1])),
                  pl.BlockSpec((1, 2, B_HEADS, NB_HALF_Q, NB_HALF_K), lambda i, t: (cls(t), 0, 0, 0, 0))],
        out_specs=pl.BlockSpec((1, NB_Q, QB_W), lambda i, t: (i, t, 0)),
        scratch_shapes=[pltpu.VMEM((NB_HALF_Q, NB_HALF_K + CTX_LEN), F32)],
        compiler_params=_cparams(("parallel", "arbitrary")),
        name="nbr_attn",
    )(slab_l, kt_l, kt_c, slab_l, slab_c, bias)


def _plain_ctx_kernel(q_ref, ktc_ref, vc_ref, o_ref):
    tiles = []
    for hd in range(B_HEADS):
        blk, half = hd // 2, hd % 2
        rows = slice(blk * LANES, (blk + 1) * LANES)
        s = _dot(_nbr_head(q_ref, hd), ktc_ref[0, 0, rows, :])
        p = jnp.exp2(s - jnp.max(s, axis=-1, keepdims=True)).astype(BF16)
        vmask = _lane_mask((KC, LANES), half * HEAD_DIM, (half + 1) * HEAD_DIM)
        vc = vc_ref[0, :, rows]
        acc = _dot(p, jnp.where(vmask, vc, jnp.ones_like(vc)))
        tiles.append(acc / pltpu.roll(acc, HEAD_DIM, 1))
        if half == 1:
            o_ref[0, :, rows] = _pair(tiles[-2], tiles[-1]).astype(BF16)


def _plain_ctx_call(ctx):
    slab_c, _, kt_c, _ = ctx
    b = slab_c.shape[0]
    return pl.pallas_call(
        _plain_ctx_kernel,
        out_shape=jax.ShapeDtypeStruct((b, CTX_LEN, QB_W), BF16),
        grid=(b,),
        in_specs=[pl.BlockSpec((1, CTX_LEN, QB_W), lambda i: (i, 0, _QB_BLK[1])),
                  pl.BlockSpec((1, 1, QB_W, KC), lambda i: (i, 0, 0, 0)),
                  pl.BlockSpec((1, CTX_LEN, QB_W), lambda i: (i, 0, _VB_BLK[1]))],
        out_specs=pl.BlockSpec((1, CTX_LEN, QB_W), lambda i: (i, 0, 0)),
        compiler_params=_cparams(("parallel",)),
        name="plain_attn_ctx",
    )(slab_c, kt_c, slab_c)


def _nbr_bias_tables(rpb):
    n_blocks = GRID_ROWS // NB_ROWS
    half_rows = NB_ROWS // 2
    sel_r = np.zeros((3, 2, half_rows, NB_HALF_KROWS, 2 * WIN_H - 1), np.float32)
    for cls, rbk in enumerate((0, 1, n_blocks - 1)):
        for g in range(2):
            ws = min(max(NB_ROWS * rbk - WIN_H // 2, 0), GRID_ROWS - NB_KROWS) + 4 * int(_nbr_half_offset(rbk, g))
            for rq in range(half_rows):
                r = NB_ROWS * rbk + g * half_rows + rq
                rs = min(max(r - WIN_H // 2, 0), GRID_ROWS - WIN_H)
                assert ws <= rs and rs + WIN_H <= ws + NB_HALF_KROWS
                for kr in range(rs - ws, rs - ws + WIN_H):
                    sel_r[cls, g, rq, kr, ws + kr - r + WIN_H - 1] = 1.0
    sel_c = np.zeros((GRID_W, GRID_W, 2 * WIN_W - 1), np.float32)
    for c in range(GRID_W):
        cs = min(max(c - WIN_W // 2, 0), GRID_W - WIN_W)
        for kc in range(cs, cs + WIN_W):
            sel_c[c, kc, kc - c + WIN_W - 1] = 1.0
    valid = np.einsum("xgqk,cd->xgqckd", sel_r.sum(-1), sel_c.sum(-1)) > 0.5
    t = jnp.einsum("xgqki,hij->xghqkj", jnp.asarray(sel_r), rpb * LOG2E, precision=lax.Precision.HIGHEST)
    t = jnp.einsum("xghqkj,cdj->xghqckd", t, jnp.asarray(sel_c), precision=lax.Precision.HIGHEST)
    t = jnp.where(jnp.asarray(valid)[:, :, None], t, NEG)
    return t.reshape(3, 2, rpb.shape[0], NB_HALF_Q, NB_HALF_K)


FFN_CHUNK = 256
HALO = SUBLANES
OHALO = 16


def _ffn_kernel(x_ref, xp_ref, xn_ref, oa_ref, oap_ref, oan_ref, ob_ref, obp_ref, obn_ref, oc_ref, ocp_ref, ocn_ref,
                mod_ref, g2_ref, wo_ref, wup_ref, cw_ref, cb_ref, wdn_ref, o_ref, y_ref, a_ref, *, T):
    t = pl.program_id(1)
    nt = pl.num_programs(1)

    def with_halo(prev, main, nxt):
        return jnp.concatenate([prev[0], main[0], nxt[0]], axis=0)

    proj = (_dot(with_halo(oap_ref, oa_ref, oan_ref), wo_ref[0:QA_W])
            + _dot(with_halo(obp_ref, ob_ref, obn_ref), wo_ref[QA_W:QA_W + QB_W])
            + _dot(with_halo(ocp_ref, oc_ref, ocn_ref), wo_ref[QA_W + QB_W:]))
    xa = (with_halo(xp_ref, x_ref, xn_ref)
          + mod_ref[0, 2:3, :] * proj[OHALO - HALO:OHALO + T + HALO])
    x = xa[HALO:HALO + T]
    ms = jnp.mean(xa * xa, axis=-1, keepdims=True)
    h = xa * lax.rsqrt(ms + EPS) * g2_ref[...]
    h = h * (1.0 + mod_ref[0, 4:5, :]) + mod_ref[0, 3:4, :]
    row = lax.broadcasted_iota(jnp.int32, (T + 2 * HALO, 1), 0)
    inside = ((row >= HALO) | (t > 0)) & ((row < T + HALO) | (t < nt - 1))
    h = jnp.where(inside, h, 0.0).astype(BF16)

    def conv(slot, col0):
        w = cw_ref[:, col0:col0 + FFN_CHUNK]
        y = y_ref[slot]
        n = y.shape[0]
        prev = pltpu.roll(y, 1, 0)[HALO:HALO + T]
        nxt = pltpu.roll(y, n - 1, 0)[HALO:HALO + T]
        return prev * w[0:1] + y[HALO:HALO + T] * w[1:2] + nxt * w[2:3] + cb_ref[:, col0:col0 + FFN_CHUNK]

    for j in range(FFN_DIM // FFN_CHUNK):
        cu, cg = j * FFN_CHUNK, FFN_DIM + j * FFN_CHUNK
        su, sg = 2 * (j % 2), 2 * (j % 2) + 1
        y_ref[su] = _dot(h, wup_ref[:, cu:cu + FFN_CHUNK])
        y_ref[sg] = _dot(h, wup_ref[:, cg:cg + FFN_CHUNK])
        g = conv(sg, cg)
        a_ref[:, cu:cu + FFN_CHUNK] = (g * (1.0 / (1.0 + jnp.exp(-g))) * conv(su, cu)).astype(BF16)
    o_ref[0] = x + mod_ref[0, 5:6, :] * _dot(a_ref[...], wdn_ref[...])


def _ffn(x, attn, mod_l, mod_row, g2, w_out, w_up, conv_w, conv_b, w_down, T):
    b, s, _ = x.shape
    const = lambda shape: pl.BlockSpec(shape, lambda i, t: (0,) * len(shape), pipeline_mode=pl.Buffered(1))

    def tile_and_halos(width, halo):
        nh, last = T // halo, s // halo - 1
        return [pl.BlockSpec((1, T, width), lambda i, t: (i, t, 0)),
                pl.BlockSpec((1, halo, width), lambda i, t: (i, jnp.maximum(t * nh - 1, 0), 0)),
                pl.BlockSpec((1, halo, width), lambda i, t: (i, jnp.minimum((t + 1) * nh, last), 0))]

    specs = tile_and_halos(D_MODEL, HALO)
    args = [x, x, x]
    for o in attn:
        specs += tile_and_halos(o.shape[-1], OHALO)
        args += [o, o, o]
    return pl.pallas_call(
        functools.partial(_ffn_kernel, T=T),
        out_shape=jax.ShapeDtypeStruct(x.shape, F32),
        grid=(b, s // T),
        in_specs=specs + [pl.BlockSpec((1, 6, D_MODEL), lambda i, t: (mod_row(i), 0, 0)),
                          const((1, D_MODEL)), const((D_MODEL, D_MODEL)), const((D_MODEL, 2 * FFN_DIM)),
                          const((3, 2 * FFN_DIM)), const((1, 2 * FFN_DIM)), const((FFN_DIM, D_MODEL))],
        out_specs=pl.BlockSpec((1, T, D_MODEL), lambda i, t: (i, t, 0)),
        scratch_shapes=[pltpu.VMEM((4, T + 2 * HALO, FFN_CHUNK), F32), pltpu.VMEM((T, FFN_DIM), BF16)],
        compiler_params=_cparams(("parallel", "arbitrary")),
        name="outproj_conv_ffn",
    )(*args, mod_l, g2, w_out, w_up, conv_w, conv_b, w_down)


def _split_pairs(w, n_groups, dim):
    lead = w.shape[:-1]
    return w.reshape(lead + (n_groups, dim // 2, 2)).swapaxes(-1, -2).reshape(lead + (n_groups * dim,))


def _rope_angles(dim):
    half = dim // 2
    inv = ROPE_THETA ** (-jnp.arange(0, half, 2, dtype=F32) / half)
    t = jnp.arange(SEQ, dtype=jnp.int32)
    row = (t // GRID_W).astype(F32)
    col = (t % GRID_W).astype(F32)
    return jnp.concatenate([row[:, None] * inv, col[:, None] * inv], axis=-1)


def _rope_tables(dim):
    ang = _rope_angles(dim)
    return jnp.cos(ang).T, jnp.sin(ang).T


def _layer_params(l, w_in, gq_a, gk_a, gq_b, gk_b, gq_c, gk_c, T):
    w = w_in[l]
    o = np.cumsum((0, QA_W, KA_W, KA_W, QB_W, QB_W, QB_W, QC_W, QC_W, QC_W))
    qa, ka, va, qb, kb, vb, qc, kc, vc = (w[:, o[i]:o[i + 1]] for i in range(9))
    wt = jnp.concatenate([_split_pairs(qa, A_HEADS, HEAD_DIM), _split_pairs(qc, 2 * C_HEADS, DIFF_DIM), kb, qb,
                          _split_pairs(kc, 2 * C_HEADS, DIFF_DIM), _split_pairs(ka, A_KV_HEADS, HEAD_DIM), va, vc, vb],
                         axis=1).T.astype(BF16)
    gqa = _split_pairs(gq_a[l], 1, HEAD_DIM) * (HEAD_DIM ** -0.5 * LOG2E)
    gka = _split_pairs(gk_a[l], 1, HEAD_DIM)
    gqc = _split_pairs(gq_c[l], 1, DIFF_DIM) * (DIFF_DIM ** -0.5 * LOG2E)
    gkc = _split_pairs(gk_c[l], 1, DIFF_DIM)
    gt = jnp.concatenate([jnp.tile(gqa, A_HEADS), jnp.tile(gqc, 2 * C_HEADS), jnp.tile(gk_b[l], B_HEADS),
                          jnp.tile(gq_b[l] * (HEAD_DIM ** -0.5 * LOG2E), B_HEADS), jnp.tile(gkc, 2 * C_HEADS),
                          jnp.tile(gka, A_KV_HEADS)])
    return wt, jnp.broadcast_to(gt[:, None], (GAIN_ROWS, T))


def kernel(x, c, ctx, c_ctx, w_ada, b_ada, g_norm1, w_in, gq_a, gk_a, gq_b, gk_b, rpb_b, gq_c, gk_c,
           lambda_q1, lambda_k1, lambda_q2, lambda_k2, g_subln, w_out, g_norm2, w_up, conv_w, conv_b, w_down):
    depth = w_ada.shape[0]
    batch = x.shape[0]
    assert x.shape == (batch, SEQ, D_MODEL) and ctx.shape == (batch, CTX_LEN, D_MODEL) and batch <= 8

    c_all = jnp.zeros((16, D_MODEL), F32).at[:batch].set(c).at[8].set(c_ctx)
    mod = _modulation(c_all, w_ada, b_ada).reshape(depth, 16, 6, D_MODEL)
    row_x = lambda i: i
    row_c = lambda i: 8

    tabs_x = _rope_tables(HEAD_DIM) + _rope_tables(DIFF_DIM)
    tabs_c = tuple(jnp.ones((t.shape[0], CTX_LEN), F32) if i % 2 == 0 else jnp.zeros((t.shape[0], CTX_LEN), F32)
                   for i, t in enumerate(tabs_x))
    T_IN = 512

    for l in range(depth):
        with_ctx = l < depth - 1
        lam_init = 0.8 - 0.6 * math.exp(-0.3 * l)
        mod_l = mod[l]
        wt, gt = _layer_params(l, w_in, gq_a, gk_a, gq_b, gk_b, gq_c, gk_c, T_IN)
        g1 = g_norm1[l].reshape(1, D_MODEL)
        g2 = g_norm2[l].reshape(1, D_MODEL)
        lam_vec = jnp.zeros((SUBLANES, LANES), F32).at[0:4, :DIFF_DIM].set(
            jnp.stack([lambda_q1[l], lambda_k1[l], lambda_q2[l], lambda_k2[l]]))
        gsub = jnp.tile(g_subln[l], 2).reshape(1, LANES)
        w_o = w_out[l].astype(BF16)
        ffn_w = (w_up[l].astype(BF16), conv_w[l], conv_b[l].reshape(1, -1), w_down[l].astype(BF16))

        lat = _inproj(x, mod_l, row_x, g1, wt, gt, tabs_x, T_IN)
        cx = _inproj(ctx, mod_l, row_c, g1, wt, gt[:, :CTX_LEN], tabs_c, CTX_LEN)

        nt_a, nt_c = SEQ // (2 * QPASS), SEQ // 512
        qn_a = pl.BlockSpec((1, QA_W, QPASS), lambda i, t: (i, 0, jnp.minimum(2 * (t + 1), 2 * nt_a - 2)))
        qn_c = pl.BlockSpec((1, LANES, 512), lambda i, t: (i, QA_W // LANES, jnp.minimum(t + 1, nt_c - 1)))
        gqa = lambda n_lat, n_pass, nxt: functools.partial(_gqa_kernel, n_lat=n_lat, n_pass=n_pass, has_next=nxt)
        dif = lambda n_lat, tq, nxt: functools.partial(_diff_kernel, n_lat=n_lat, Tq=tq, lam_init=lam_init,
                                                       has_next=nxt)
        oa = _flash_call(gqa(SEQ // KC, 2, True), "gqa_attn", _QTA_BLK, qn_a, _KA_BLK, _VTA_BLK, QA_W, 2 * QPASS,
                         A_KV_HEADS, A_REP * QPASS, lat, lat, cx)
        oc = _flash_call(dif(SEQ // KC, 512, True), "diff_attn", _QTC_BLK, qn_c, _KC_BLK, _VTC_BLK, QC_W, 512,
                         2, 2 * 512, lat, lat, cx, (lam_vec, gsub))
        ob = _nbr_call(lat, cx, _nbr_bias_tables(rpb_b[l]))
        x = _ffn(x, (oa, ob, oc), mod_l, row_x, g2, w_o, *ffn_w, 512)

        if with_ctx:
            ta = _flash_call(gqa(0, 1, False), "gqa_attn_ctx", _QTA_BLK, None, _KA_BLK, _VTA_BLK, QA_W, CTX_LEN,
                             A_KV_HEADS, A_REP * CTX_LEN, cx, None, cx)
            tc = _flash_call(dif(0, CTX_LEN, False), "diff_attn_ctx", _QTC_BLK, None, _KC_BLK, _VTC_BLK, QC_W,
                             CTX_LEN, 2, 2 * CTX_LEN, cx, None, cx, (lam_vec, gsub))
            tb = _plain_ctx_call(cx)
            ctx = _ffn(ctx, (ta, tb, tc), mod_l, row_c, g2, w_o, *ffn_w, CTX_LEN)
    return x
```
